```python
import jax, jax.numpy as jnp
from jax import lax
import numpy as np

D_MODEL = 1024
BATCH = 8
SEQ = 2048
DEPTH = 4
DEC_BATCH = 32
DEC_SEQ = 1
PAST_LEN = 8192
PAGE_SIZE = 128

N_BRANCH = 4
BRANCH_WIDTH = D_MODEL // 2
H_A = 4
DV_A = BRANCH_WIDTH // H_A
DK_A = DV_A // 2
GLA_RANK = 16
GLA_TAU = 16.0
H_B = 4
DV_B = BRANCH_WIDTH // H_B
DF_B = D_MODEL // 16
POOL_WIDTH = BRANCH_WIDTH
POOL_WINDOWS = (2, 4, 8, 16)
POOL_GROUP = POOL_WIDTH // len(POOL_WINDOWS)
POOL_HIST = max(POOL_WINDOWS) - 1
H_D = 8
HD_D = BRANCH_WIDTH // H_D
KVH_D = 2
G_D = H_D // KVH_D
CMP_STRIDE = 16
CMP_BLOCK = 2 * CMP_STRIDE
SEL_BLOCK = 4 * CMP_STRIDE
TOPK = 16
WINDOW = 512
Q_BLOCK = 32
FORCE_BONUS = 1000.0
CHUNK = 16
D_FF = 4 * D_MODEL
EPS = 1e-6
NEG = -1e30
IN_SIZES = (H_A * DK_A, H_A * DK_A, H_A * DV_A, H_A * DV_A, GLA_RANK,
            H_B * DF_B, H_B * DF_B, H_B * DV_B, H_B * DV_B,
            POOL_WIDTH,
            H_D * HD_D, KVH_D * HD_D, KVH_D * HD_D, KVH_D * HD_D, KVH_D * HD_D, KVH_D * HD_D, KVH_D * HD_D, H_D * 3)
IN_WIDTH = sum(IN_SIZES)

kernel_name = 'hybrid_gla_hgrn2_pool_nsa_step'


def rms_norm(x, g):
    xf = x.astype(jnp.float32)
    y = xf * lax.rsqrt(jnp.mean(xf * xf, axis=-1, keepdims=True) + EPS)
    return (y * g.astype(jnp.float32)).astype(x.dtype)


def masked_softmax(s, mask):
    p = jax.nn.softmax(jnp.where(mask, s, NEG), axis=-1)
    return jnp.where(mask, p, 0.0)


def split_cols(z):
    offs, c = [], 0
    for s in IN_SIZES[:-1]:
        c += s
        offs.append(c)
    return jnp.split(z, offs, axis=-1)


def chunked_gla(q, k, v, log_f, s0):
    f32 = jnp.float32
    B, L, H, K = q.shape
    pad = (-L) % CHUNK
    n = (L + pad) // CHUNK
    pw = ((0, 0), (0, pad), (0, 0), (0, 0))

    def blocks(a):
        return jnp.pad(a.astype(f32), pw).reshape(B, n, CHUNK, H, a.shape[-1])

    q, k, v, g = blocks(q), blocks(k), blocks(v), blocks(log_f)
    b = jnp.cumsum(g, axis=2)
    causal = jnp.tril(jnp.ones((CHUNK, CHUNK), dtype=bool))[None, None, :, :, None, None]
    decay = jnp.exp(jnp.where(causal, b[:, :, :, None] - b[:, :, None, :], -jnp.inf))
    attn = jnp.einsum('bnthk,bnshk,bntshk->bnhts', q, k, decay)
    o_intra = jnp.einsum('bnhts,bnshv->bnthv', attn, v)
    b_last = b[:, :, -1]
    q_dec = q * jnp.exp(b)
    d_state = jnp.einsum('bnshk,bnshv->bnhkv', k * jnp.exp(b_last[:, :, None] - b), v)

    def step(S, inp):
        qd, ds, bl = inp
        o = jnp.einsum('bthk,bhkv->bthv', qd, S)
        return S * jnp.exp(bl)[..., None] + ds, o

    s_fin, o_inter = lax.scan(step, s0.astype(f32),
                              (jnp.moveaxis(q_dec, 1, 0), jnp.moveaxis(d_state, 1, 0), jnp.moveaxis(b_last, 1, 0)))
    o = o_intra + jnp.moveaxis(o_inter, 0, 1)
    return o.reshape(B, n * CHUNK, H, -1)[:, :L], s_fin


def pool_mixer(u, prev, pos0, w_grp, scale):
    f32 = jnp.float32
    B, L, C = u.shape
    ext = jnp.concatenate([prev.astype(f32), u.astype(f32)], axis=1)
    cs = jnp.pad(jnp.cumsum(ext, axis=1), ((0, 0), (1, 0), (0, 0)))
    n_avail = pos0 + jnp.arange(L) + 1
    means = []
    for gi, w in enumerate(POOL_WINDOWS):
        c0 = gi * POOL_GROUP
        lo = POOL_HIST + 1 - w
        win_sum = cs[:, POOL_HIST + 1:POOL_HIST + 1 + L, c0:c0 + POOL_GROUP] - cs[:, lo:lo + L, c0:c0 + POOL_GROUP]
        means.append(win_sum / jnp.minimum(w, n_avail).astype(f32)[None, :, None])
    d = jnp.stack(means, axis=2) - u.astype(f32).reshape(B, L, len(POOL_WINDOWS), POOL_GROUP)
    y = jnp.einsum('blgc,gcd->blgd', d, w_grp.astype(f32)).reshape(B, L, C) * scale.astype(f32)
    return y, ext[:, ext.shape[1] - POOL_HIST:]


def nsa_attend(q, gates, kv_cmp, kv_slc, kv_win, pos0, kc_gain):
    f32 = jnp.float32
    B, Lq = q.shape[0], q.shape[1]
    Lk = kv_cmp.shape[1]
    pad = (-Lk) % SEL_BLOCK
    pk = ((0, 0), (0, pad), (0, 0), (0, 0), (0, 0))
    kv_cmp = jnp.pad(kv_cmp.astype(f32), pk)
    kv_slc = jnp.pad(kv_slc.astype(f32), pk)
    lp = Lk + pad
    n16 = lp // CMP_STRIDE
    s16 = kv_cmp.reshape(B, n16, CMP_STRIDE, 2, KVH_D, HD_D).sum(axis=2)
    cmp = (s16[:, :-1] + s16[:, 1:]) / CMP_BLOCK
    kc = rms_norm(cmp[:, :, 0], kc_gain)
    vc = cmp[:, :, 1]
    n_cmp = n16 - 1
    c_end = jnp.arange(n_cmp) * CMP_STRIDE + (CMP_BLOCK - 1)
    n_sel = lp // SEL_BLOCK
    per = SEL_BLOCK // CMP_STRIDE
    ci = jnp.arange(n_cmp)[:, None]
    sj = jnp.arange(n_sel)[None, :]
    overlap = ((ci // per == sj) | ((ci + 1) // per == sj)).astype(f32)
    kb = kv_slc[:, :, 0].reshape(B, n_sel, SEL_BLOCK, KVH_D, HD_D).transpose(0, 3, 1, 2, 4)
    vb = kv_slc[:, :, 1].reshape(B, n_sel, SEL_BLOCK, KVH_D, HD_D).transpose(0, 3, 1, 2, 4)
    k_sel = min(TOPK, n_sel)
    qb_len = Q_BLOCK if Lq % Q_BLOCK == 0 else Lq
    qf = q.astype(f32) * (HD_D ** -0.5)
    gf = gates.astype(f32)
    kvw = kv_win.astype(f32)
    gather = jax.vmap(jax.vmap(lambda blk, ix: blk[ix]))
    blk_ids = jnp.arange(n_sel)[None, :]

    def one_block(i):
        s = i * qb_len
        qb = lax.dynamic_slice_in_dim(qf, s, qb_len, axis=1)
        gb = lax.dynamic_slice_in_dim(gf, s, qb_len, axis=1)
        t = pos0 + s + jnp.arange(qb_len)
        sc = jnp.einsum('bqhgd,bchd->bqhgc', qb, kc)
        pc = masked_softmax(sc, (c_end[None, :] <= t[:, None])[None, :, None, None, :])
        o_cmp = jnp.einsum('bqhgc,bchd->bqhgd', pc, vc)
        imp = jnp.einsum('bqhgc,cn->bqhn', pc, overlap)
        cur = (t // SEL_BLOCK)[:, None]
        forced = (blk_ids == 0) | (blk_ids == cur) | (blk_ids == cur - 1)
        bonus = jnp.where(forced, FORCE_BONUS, 0.0)[None, :, None, :]
        score = jnp.where((blk_ids <= cur)[None, :, None, :], imp + bonus, -jnp.inf)
        _, idx = lax.top_k(score, k_sel)
        idx_h = idx.transpose(0, 2, 1, 3)
        ks = gather(kb, idx_h)
        vs = gather(vb, idx_h)
        ss = jnp.einsum('bqhgd,bhqkpd->bqhgkp', qb, ks)
        kpos = idx[..., None] * SEL_BLOCK + jnp.arange(SEL_BLOCK)
        smask = (kpos <= t[None, :, None, None, None])[:, :, :, None]
        shp = ss.shape
        ps = masked_softmax(ss.reshape(shp[:4] + (-1,)), smask.reshape(smask.shape[:4] + (-1,))).reshape(shp)
        o_slc = jnp.einsum('bqhgkp,bhqkpd->bqhgd', ps, vs)
        wb = lax.dynamic_slice_in_dim(kvw, s, qb_len + WINDOW, axis=1)
        wpos = pos0 - WINDOW + s + jnp.arange(qb_len + WINDOW)
        dist = t[:, None] - wpos[None, :]
        wmask = ((dist >= 0) & (dist <= WINDOW) & (wpos >= 0)[None, :])[None, :, None, None, :]
        sw = jnp.einsum('bqhgd,bkhd->bqhgk', qb, wb[:, :, 0])
        o_win = jnp.einsum('bqhgk,bkhd->bqhgd', masked_softmax(sw, wmask), wb[:, :, 1])
        return gb[..., 0:1] * o_cmp + gb[..., 1:2] * o_slc + gb[..., 2:3] * o_win

    o = lax.map(one_block, jnp.arange(Lq // qb_len))
    return jnp.moveaxis(o, 0, 1).reshape(B, Lq, H_D * HD_D)


def trunk_layer(x, pos0, gla_s0, hgrn_s0, pool_prev, cmp_past, slc_past, win_prior, lw):
    (n1, w_in, gla_wa2, gla_ba, gla_ng, lb, hgrn_ng, pool_w, pool_sc, q_ng, k_ng,
     w_branch, w_gate, b_gate, w_out, n2, w1, w2) = lw
    f32 = jnp.float32
    dt = x.dtype
    B, L = x.shape[0], x.shape[1]
    heads = lambda a, n: a.reshape(B, L, n, -1)
    h = rms_norm(x, n1)
    z = jnp.einsum('bld,de->ble', h, w_in)
    (gq, gk, gv, gr, ga, hq, hf, hi, hg, pu, nq, ck, cv, sk, sv, wk, wv, ng) = split_cols(z)
    log_a = jax.nn.log_sigmoid(jnp.einsum('blr,re->ble', ga, gla_wa2).astype(f32) + gla_ba.astype(f32)) / GLA_TAU
    o, gla_state = chunked_gla(heads(gq, H_A) * (DK_A ** -0.5), heads(gk, H_A), heads(gv, H_A), heads(log_a, H_A), gla_s0)
    o_gla = (rms_norm(o, gla_ng) * jax.nn.silu(heads(gr, H_A).astype(f32))).reshape(B, L, -1)
    fgate = lb + (1.0 - lb) * jax.nn.sigmoid(hf.astype(f32))
    o, hgrn_state = chunked_gla(heads(jax.nn.silu(hq.astype(f32)), H_B) * (DF_B ** -0.5), heads(1.0 - fgate, H_B),
                                heads(hi, H_B), heads(jnp.log(fgate), H_B), hgrn_s0)
    o_hgrn = (rms_norm(o, hgrn_ng) * jax.nn.silu(heads(hg, H_B).astype(f32))).reshape(B, L, -1)
    o_pool, pool_state = pool_mixer(pu, pool_prev, pos0, pool_w, pool_sc)
    q = rms_norm(heads(nq, H_D), q_ng).reshape(B, L, KVH_D, G_D, HD_D)
    kv_cmp_new = jnp.stack([heads(ck, KVH_D), heads(cv, KVH_D)], axis=2)
    kv_slc_new = jnp.stack([rms_norm(heads(sk, KVH_D), k_ng[1]), heads(sv, KVH_D)], axis=2)
    kv_win_new = jnp.stack([rms_norm(heads(wk, KVH_D), k_ng[2]), heads(wv, KVH_D)], axis=2)
    nsa_gates = jax.nn.sigmoid(ng.astype(f32)).reshape(B, L, KVH_D, G_D, 3)
    pad_w = WINDOW - win_prior.shape[1]
    win_all = jnp.concatenate([jnp.pad(win_prior.astype(dt), ((0, 0), (pad_w, 0), (0, 0), (0, 0), (0, 0))), kv_win_new], axis=1)
    o_nsa = nsa_attend(q, nsa_gates,
                       jnp.concatenate([cmp_past.astype(dt), kv_cmp_new], axis=1),
                       jnp.concatenate([slc_past.astype(dt), kv_slc_new], axis=1),
                       win_all, pos0, k_ng[0])
    branches = jnp.stack([o_gla, o_hgrn, o_pool, o_nsa], axis=2).astype(dt)
    proj = jnp.einsum('blnc,ncd->blnd', branches, w_branch)
    gate = jax.nn.sigmoid(jnp.einsum('bld,de->ble', h, w_gate) + b_gate).reshape(B, L, N_BRANCH, -1)
    x = x + jnp.einsum('bld,de->ble', jnp.sum(gate * proj, axis=2), w_out)
    u = jax.nn.relu(jnp.einsum('bld,df->blf', rms_norm(x, n2), w1))
    x = x + jnp.einsum('blf,fd->bld', u * u, w2)
    return x, gla_state, hgrn_state, pool_state.astype(dt), kv_cmp_new, kv_slc_new, kv_win_new


def setup_inputs(seed: int = 0) -> dict:
    key = jax.random.key(seed)
    ks = jax.random.split(key, 32)
    f32 = jnp.float32
    n_pages = PAST_LEN // PAGE_SIZE
    n_phys = (DEC_BATCH * n_pages * 5) // 4
    win_buf = min(WINDOW, PAST_LEN)
    kv_row = (2, KVH_D, HD_D)

    def nrm(i, shape, scale=1.0):
        return jax.random.normal(ks[i], shape, f32) * scale

    def gain(i, shape):
        return 1.0 + nrm(i, shape, 0.02)

    page_table = jax.random.permutation(ks[9], n_phys)[:DEC_BATCH * n_pages].reshape(DEC_BATCH, n_pages).astype(jnp.int32)
    return {
        'x_prompt': nrm(0, (BATCH, SEQ, D_MODEL)),
        'x_sample': nrm(1, (DEC_BATCH, DEC_SEQ, D_MODEL)),
        'state_gla': nrm(2, (DEPTH, DEC_BATCH, H_A, DK_A, DV_A)),
        'state_hgrn': nrm(3, (DEPTH, DEC_BATCH, H_B, DF_B, DV_B)),
        'state_pool': nrm(4, (DEPTH, DEC_BATCH, POOL_HIST, POOL_WIDTH)),
        'cache_cmp_kv': nrm(5, (DEPTH, n_phys, PAGE_SIZE) + kv_row),
        'cache_slc_kv': nrm(6, (DEPTH, n_phys, PAGE_SIZE) + kv_row),
        'cache_win_kv': nrm(7, (DEPTH, DEC_BATCH, win_buf) + kv_row),
        'page_table': page_table,
        'norm1_g': gain(10, (DEPTH, D_MODEL)),
        'w_in': nrm(11, (DEPTH, D_MODEL, IN_WIDTH), D_MODEL ** -0.5),
        'gla_wa2': nrm(12, (DEPTH, GLA_RANK, H_A * DK_A), GLA_RANK ** -0.5),
        'gla_ba': nrm(13, (DEPTH, H_A * DK_A), 0.1),
        'gla_norm_g': gain(14, (DEPTH, DV_A)),
        'hgrn_lb_param': nrm(15, (DEPTH, H_B * DF_B), 0.1),
        'hgrn_norm_g': gain(16, (DEPTH, DV_B)),
        'pool_w': nrm(17, (DEPTH, len(POOL_WINDOWS), POOL_GROUP, POOL_GROUP), POOL_GROUP ** -0.5),
        'pool_scale': gain(18, (DEPTH, POOL_WIDTH)),
        'nsa_q_norm_g': gain(19, (DEPTH, HD_D)),
        'nsa_k_norm_g': gain(20, (DEPTH, 3, HD_D)),
        'w_branch': nrm(21, (DEPTH, N_BRANCH, BRANCH_WIDTH, D_MODEL), BRANCH_WIDTH ** -0.5),
        'w_gate': nrm(22, (DEPTH, D_MODEL, N_BRANCH * D_MODEL), D_MODEL ** -0.5),
        'b_gate': nrm(23, (DEPTH, N_BRANCH * D_MODEL), 0.02),
        'w_out': nrm(24, (DEPTH, D_MODEL, D_MODEL), D_MODEL ** -0.5),
        'norm2_g': gain(25, (DEPTH, D_MODEL)),
        'mlp_w1': nrm(26, (DEPTH, D_MODEL, D_FF), D_MODEL ** -0.5),
        'mlp_w2': nrm(27, (DEPTH, D_FF, D_MODEL), D_FF ** -0.5),
    }


def reference(x_prompt, x_sample, state_gla, state_hgrn, state_pool, cache_cmp_kv, cache_slc_kv, cache_win_kv,
              page_table, norm1_g, w_in, gla_wa2, gla_ba, gla_norm_g, hgrn_lb_param, hgrn_norm_g, pool_w,
              pool_scale, nsa_q_norm_g, nsa_k_norm_g, w_branch, w_gate, b_gate, w_out, norm2_g, mlp_w1, mlp_w2):
    f32 = jnp.float32
    B, L = x_prompt.shape[0], x_prompt.shape[1]
    Bd = x_sample.shape[0]
    dt = x_prompt.dtype
    lb_soft = jax.nn.softmax(hgrn_lb_param.astype(f32), axis=0)
    lb_all = jnp.cumsum(lb_soft, axis=0) - lb_soft[0]
    keep_p = min(WINDOW, L)
    xp, xs = x_prompt, x_sample
    gla_p, gla_s, hgrn_p, hgrn_s, pool_p, pool_s = [], [], [], [], [], []
    cmp_p, cmp_s, slc_p, slc_s, win_p, win_s = [], [], [], [], [], []
    for l in range(DEPTH):
        lw = (norm1_g[l], w_in[l], gla_wa2[l], gla_ba[l], gla_norm_g[l], lb_all[l], hgrn_norm_g[l], pool_w[l],
              pool_scale[l], nsa_q_norm_g[l], nsa_k_norm_g[l], w_branch[l], w_gate[l], b_gate[l], w_out[l],
              norm2_g[l], mlp_w1[l], mlp_w2[l])
        xp, g_p, h_p, p_p, c_p, s_p, w_p = trunk_layer(
            xp, 0, jnp.zeros((B, H_A, DK_A, DV_A), f32), jnp.zeros((B, H_B, DF_B, DV_B), f32),
            jnp.zeros((B, POOL_HIST, POOL_WIDTH), dt), jnp.zeros((B, 0, 2, KVH_D, HD_D), dt),
            jnp.zeros((B, 0, 2, KVH_D, HD_D), dt), jnp.zeros((B, WINDOW, 2, KVH_D, HD_D), dt), lw)
        cmp_past = cache_cmp_kv[l][page_table].reshape(Bd, -1, 2, KVH_D, HD_D)
        slc_past = cache_slc_kv[l][page_table].reshape(Bd, -1, 2, KVH_D, HD_D)
        xs, g_s, h_s, p_s, c_s, s_s, w_s = trunk_layer(
            xs, PAST_LEN, state_gla[l], state_hgrn[l], state_pool[l], cmp_past, slc_past, cache_win_kv[l], lw)
        gla_p.append(g_p); gla_s.append(g_s); hgrn_p.append(h_p); hgrn_s.append(h_s)
        pool_p.append(p_p); pool_s.append(p_s); cmp_p.append(c_p); cmp_s.append(c_s)
        slc_p.append(s_p); slc_s.append(s_s); win_p.append(w_p[:, L - keep_p:]); win_s.append(w_s)
    return (xp, xs,
            jnp.stack(gla_p), jnp.stack(gla_s), jnp.stack(hgrn_p), jnp.stack(hgrn_s),
            jnp.stack(pool_p), jnp.stack(pool_s), jnp.stack(cmp_p), jnp.stack(cmp_s),
            jnp.stack(slc_p), jnp.stack(slc_s), jnp.stack(win_p), jnp.stack(win_s))
```

```python
import functools

import jax
import jax.numpy as jnp
from jax import lax
from jax.experimental import pallas as pl
from jax.experimental.pallas import tpu as pltpu

F32 = jnp.float32
BF16 = jnp.bfloat16
I32 = jnp.int32

N_BRANCH = 4
H_LIN = 4
DK_LIN = 64
DV_LIN = 128
GLA_RANK = 16
GLA_TAU = 16.0
POOL_WINDOWS = (2, 4, 8, 16)
POOL_HIST = 15
POOL_GROUP = 128
H_D = 8
HD_D = 64
KVH_D = 2
G_D = 4
CMP_STRIDE = 16
CMP_BLOCK = 32
SEL_BLOCK = 64
SEL_SHIFT = 6
TOPK = 16
WINDOW = 512
FORCE_BONUS = 1000.0
PAGE_SIZE = 128
EPS = 1e-6
NEG = -1e30

LANES = 128
VMEM_LIMIT = 56 * 1024 * 1024

QK = H_LIN * DK_LIN
VW = H_LIN * DV_LIN
OFF_GLA = 0
OFF_HGRN = OFF_GLA + 2 * QK + 2 * VW
OFF_POOL = OFF_HGRN + 2 * QK + 2 * VW
OFF_NQ = OFF_POOL + 512
OFF_KV = OFF_NQ + H_D * HD_D
OFF_TAIL = OFF_KV + 6 * KVH_D * HD_D
IN_PAD = OFF_TAIL + LANES
TAIL_GATE0 = GLA_RANK
KV_ROW = 2 * KVH_D * HD_D


def _dot(a, b):
    return jnp.dot(a, b, preferred_element_type=F32)


def _dot_nt(a, b):
    return lax.dot_general(a, b, (((1,), (1,)), ((), ())), preferred_element_type=F32)


def _rms(x, gain):
    return x * lax.rsqrt(jnp.mean(x * x, axis=-1, keepdims=True) + EPS) * gain


def _seg_mean(x2, seg):
    parts = []
    for c in range(x2.shape[1] // LANES):
        xc = x2[:, c * LANES:(c + 1) * LANES]
        h1 = xc.astype(BF16)
        r1 = xc - h1.astype(F32)
        h2 = r1.astype(BF16)
        h3 = (r1 - h2.astype(F32)).astype(BF16)
        parts.append(_dot(h1, seg) + _dot(h2, seg) + _dot(h3, seg))
    return parts[0] if len(parts) == 1 else jnp.concatenate(parts, axis=1)


def _seg_rms(x, gain, seg):
    return x * lax.rsqrt(_seg_mean(x * x, seg) + EPS) * gain


def _sigmoid(x):
    return 1.0 / (1.0 + jnp.exp(-x))


def _in_proj_kernel(layer, x_ref, n1_ref, w_ref, wa2_ref, ba_ref, lbp_ref, qg_ref, kg_ref, seg_ref,
                    gqkg_ref, gvr_ref, hqkg_ref, hvr_ref, u_ref, nq_ref, cmp_ref, slc_ref, win_ref, tail_ref):
    x = x_ref[...]
    hb = _rms(x, n1_ref[...]).astype(BF16)
    seg = seg_ref[...]

    def proj(c0, c1):
        return _dot(hb, w_ref[:, c0:c1])

    tail = proj(OFF_TAIL, IN_PAD)
    tail_ref[...] = _sigmoid(tail)

    a_pre = _dot(tail.astype(BF16), wa2_ref[...]) + ba_ref[...]
    log_a = (jnp.minimum(a_pre, 0.0) - jnp.log(1.0 + jnp.exp(-jnp.abs(a_pre)))) / GLA_TAU
    z = proj(OFF_GLA, OFF_GLA + 2 * QK)
    gqkg_ref[:, 0:QK] = z[:, 0:QK] * (DK_LIN ** -0.5)
    gqkg_ref[:, QK:2 * QK] = z[:, QK:2 * QK]
    gqkg_ref[:, 2 * QK:3 * QK] = log_a
    gvr_ref[...] = proj(OFF_GLA + 2 * QK, OFF_HGRN)

    lbp = lbp_ref[...]
    pe = jnp.exp(lbp - jnp.max(lbp, axis=0, keepdims=True))
    soft = pe / jnp.sum(pe, axis=0, keepdims=True)
    lb = jnp.zeros((1, QK), F32)
    for i in range(1, layer + 1):
        lb = lb + soft[i:i + 1, :]
    z = proj(OFF_HGRN, OFF_HGRN + 2 * QK)
    hq = z[:, 0:QK]
    fgate = lb + (1.0 - lb) * _sigmoid(z[:, QK:2 * QK])
    hqkg_ref[:, 0:QK] = hq * _sigmoid(hq) * (DK_LIN ** -0.5)
    hqkg_ref[:, QK:2 * QK] = 1.0 - fgate
    hqkg_ref[:, 2 * QK:3 * QK] = jnp.log(fgate)
    hvr_ref[...] = proj(OFF_HGRN + 2 * QK, OFF_POOL)

    u_ref[...] = proj(OFF_POOL, OFF_NQ)
    nq_ref[...] = _seg_rms(proj(OFF_NQ, OFF_KV), qg_ref[...], seg)

    zkv = proj(OFF_KV, OFF_TAIL)
    cmp_ref[...] = zkv[:, 0:KV_ROW]
    slc_ref[:, 0:LANES] = _seg_rms(zkv[:, 256:384], kg_ref[1:2, :], seg)
    slc_ref[:, LANES:KV_ROW] = zkv[:, 384:512]
    win_ref[:, 0:LANES] = _seg_rms(zkv[:, 512:640], kg_ref[2:3, :], seg)
    win_ref[:, LANES:KV_ROW] = zkv[:, 640:768]


def _in_proj(x2d, layer, P, tm):
    T, D = x2d.shape
    depth = P['lbp'].shape[0]
    row = lambda w: pl.BlockSpec((tm, w), lambda i: (i, 0))
    lay2 = lambda a, b: pl.BlockSpec((None, a, b), lambda i: (layer, 0, 0))
    widths = (3 * QK, 2 * VW, 3 * QK, 2 * VW, 512, H_D * HD_D, KV_ROW, KV_ROW, KV_ROW, LANES)
    return pl.pallas_call(
        functools.partial(_in_proj_kernel, layer),
        grid=(T // tm,),
        in_specs=[row(D), lay2(1, D), lay2(D, IN_PAD), lay2(LANES, QK), lay2(1, QK),
                  pl.BlockSpec((depth, QK), lambda i: (0, 0)), lay2(1, H_D * HD_D), lay2(3, LANES),
                  pl.BlockSpec((LANES, LANES), lambda i: (0, 0))],
        out_specs=[row(w) for w in widths],
        out_shape=[jax.ShapeDtypeStruct((T, w), F32) for w in widths],
        compiler_params=pltpu.CompilerParams(dimension_semantics=("parallel",), vmem_limit_bytes=VMEM_LIMIT),
        name="in_proj",
    )(x2d, P['n1'], P['w_in'], P['wa2'], P['ba'], P['lbp'], P['qg'], P['kg'], P['seg'])


def _gla_kernel(levels, qkg_ref, vr_ref, gain_ref, o_ref, s_ref):
    CB = qkg_ref.shape[1]

    @pl.when(pl.program_id(1) == 0)
    def _():
        s_ref[...] = jnp.zeros_like(s_ref)

    q = qkg_ref[0, :, 0:QK]
    k = qkg_ref[0, :, QK:2 * QK]
    g = qkg_ref[0, :, 2 * QK:3 * QK]
    row = lax.broadcasted_iota(I32, (CB, 1), 0)
    r2 = lax.broadcasted_iota(I32, (CB, CB), 0)
    c2 = lax.broadcasted_iota(I32, (CB, CB), 1)
    heads = [slice(h * DK_LIN, (h + 1) * DK_LIN) for h in range(H_LIN)]

    qb = q.astype(BF16)
    kb = k.astype(BF16)
    att = [jnp.where(r2 == c2, _dot_nt(qb[:, hs], kb[:, hs]), 0.0) for hs in heads]
    c = g
    tot = g
    for lvl in range(levels):
        half = 1 << lvl
        upper = ((row >> lvl) & 1) == 1
        qt = jnp.where(upper, q * jnp.exp(c), 0.0).astype(BF16)
        kt = jnp.where(upper, 0.0, k * jnp.exp(tot - c)).astype(BF16)
        same = (r2 >> (lvl + 1)) == (c2 >> (lvl + 1))
        for h, hs in enumerate(heads):
            att[h] = att[h] + jnp.where(same, _dot_nt(qt[:, hs], kt[:, hs]), 0.0)
        t_lo = pltpu.roll(tot, half, 0)
        t_hi = pltpu.roll(tot, CB - half, 0)
        c = c + jnp.where(upper, t_lo, 0.0)
        tot = tot + jnp.where(upper, t_lo, t_hi)

    qd = (q * jnp.exp(c)).astype(BF16)
    kdt = jnp.transpose(k * jnp.exp(tot - c)).astype(BF16)
    eye = (lax.broadcasted_iota(I32, (DK_LIN, DK_LIN), 0) == lax.broadcasted_iota(I32, (DK_LIN, DK_LIN), 1))
    gain = gain_ref[...]
    for h, hs in enumerate(heads):
        vs = slice(h * DV_LIN, (h + 1) * DV_LIN)
        v = vr_ref[0, :, vs].astype(BF16)
        s_old = s_ref[0, h]
        o = _dot(att[h].astype(BF16), v) + _dot(qd[:, hs], s_old.astype(BF16))
        t_row = jnp.broadcast_to(tot[0:1, hs], (DK_LIN, DK_LIN))
        t_col = jnp.sum(jnp.where(eye, t_row, 0.0), axis=1, keepdims=True)
        s_ref[0, h] = s_old * jnp.exp(t_col) + _dot(kdt[hs, :], v)
        r = vr_ref[0, :, VW + h * DV_LIN:VW + (h + 1) * DV_LIN]
        o_ref[0, :, vs] = _rms(o, gain) * (r * _sigmoid(r))


def _gla_prompt(qkg, vr, gain, cb):
    B, L, _ = qkg.shape
    levels = cb.bit_length() - 1
    return pl.pallas_call(
        functools.partial(_gla_kernel, levels),
        grid=(B, L // cb),
        in_specs=[pl.BlockSpec((1, cb, 3 * QK), lambda b, j: (b, j, 0)),
                  pl.BlockSpec((1, cb, 2 * VW), lambda b, j: (b, j, 0)),
                  pl.BlockSpec((1, DV_LIN), lambda b, j: (0, 0))],
        out_specs=[pl.BlockSpec((1, cb, VW), lambda b, j: (b, j, 0)),
                   pl.BlockSpec((1, H_LIN, DK_LIN, DV_LIN), lambda b, j: (b, 0, 0, 0))],
        out_shape=[jax.ShapeDtypeStruct((B, L, VW), F32),
                   jax.ShapeDtypeStruct((B, H_LIN, DK_LIN, DV_LIN), F32)],
        compiler_params=pltpu.CompilerParams(dimension_semantics=("parallel", "arbitrary"),
                                             vmem_limit_bytes=VMEM_LIMIT),
        name="gla_prompt",
    )(qkg, vr, gain)


def _gla_dec_kernel(q_ref, k_ref, g_ref, v_ref, r_ref, gain_ref, s0_ref, o_ref, s_ref):
    s_new = s0_ref[...] * jnp.exp(g_ref[...]) + k_ref[...] * v_ref[...]
    s_ref[...] = s_new
    o = jnp.sum(q_ref[...] * s_new, axis=2, keepdims=True)
    r = r_ref[...]
    o_ref[...] = _rms(o, gain_ref[...]) * (r * _sigmoid(r))


def _gla_decode(qkg, vr, gain, state, layer, bb):
    Bd = qkg.shape[0]
    col = lambda a: a.reshape(Bd, H_LIN, DK_LIN, 1)
    rowv = lambda a: a.reshape(Bd, H_LIN, 1, DV_LIN)
    q, k, g = col(qkg[:, 0:QK]), col(qkg[:, QK:2 * QK]), col(qkg[:, 2 * QK:])
    v, r = rowv(vr[:, 0:VW]), rowv(vr[:, VW:])
    cspec = pl.BlockSpec((bb, H_LIN, DK_LIN, 1), lambda i: (i, 0, 0, 0))
    rspec = pl.BlockSpec((bb, H_LIN, 1, DV_LIN), lambda i: (i, 0, 0, 0))
    o, s_new = pl.pallas_call(
        _gla_dec_kernel,
        grid=(Bd // bb,),
        in_specs=[cspec, cspec, cspec, rspec, rspec,
                  pl.BlockSpec((1, 1, 1, DV_LIN), lambda i: (0, 0, 0, 0)),
                  pl.BlockSpec((None, bb, H_LIN, DK_LIN, DV_LIN), lambda i: (layer, i, 0, 0, 0))],
        out_specs=[rspec, pl.BlockSpec((bb, H_LIN, DK_LIN, DV_LIN), lambda i: (i, 0, 0, 0))],
        out_shape=[jax.ShapeDtypeStruct((Bd, H_LIN, 1, DV_LIN), F32),
                   jax.ShapeDtypeStruct((Bd, H_LIN, DK_LIN, DV_LIN), F32)],
        compiler_params=pltpu.CompilerParams(dimension_semantics=("parallel",), vmem_limit_bytes=VMEM_LIMIT),
        name="gla_decode",
    )(q, k, g, v, r, gain.reshape(1, 1, 1, DV_LIN), state)
    return o.reshape(Bd, VW), s_new


def _pool_kernel(pos0, rt, u_ref, prev_ref, w_ref, sc_ref, o_ref, ext_ref):
    L = u_ref.shape[1]
    hdr = POOL_HIST + 1
    ext_ref[0:hdr, :] = prev_ref[0]
    ext_ref[hdr:hdr + L, :] = u_ref[0]
    for r0 in range(0, L, rt):
        n_avail = (pos0 + r0 + 1 + lax.broadcasted_iota(I32, (rt, 1), 0)).astype(F32)
        for gi, w in enumerate(POOL_WINDOWS):
            cs = slice(gi * POOL_GROUP, (gi + 1) * POOL_GROUP)
            tok = ext_ref[hdr + r0:hdr + r0 + rt, cs]
            acc = tok
            for j in range(1, w):
                acc = acc + ext_ref[hdr + r0 - j:hdr + r0 - j + rt, cs]
            d = acc / jnp.minimum(float(w), n_avail) - tok
            o_ref[0, r0:r0 + rt, cs] = _dot(d.astype(BF16), w_ref[gi]) * sc_ref[:, cs]


def _pool(u, prev16, layer, P, pos0):
    B, L, C = u.shape
    rt = min(L, 256)
    return pl.pallas_call(
        functools.partial(_pool_kernel, pos0, rt),
        grid=(B,),
        in_specs=[pl.BlockSpec((1, L, C), lambda b: (b, 0, 0)),
                  pl.BlockSpec((1, POOL_HIST + 1, C), lambda b: (b, 0, 0)),
                  pl.BlockSpec((None, len(POOL_WINDOWS), POOL_GROUP, POOL_GROUP), lambda b: (layer, 0, 0, 0)),
                  pl.BlockSpec((None, 1, C), lambda b: (layer, 0, 0))],
        out_specs=pl.BlockSpec((1, L, C), lambda b: (b, 0, 0)),
        out_shape=jax.ShapeDtypeStruct((B, L, C), F32),
        scratch_shapes=[pltpu.VMEM((POOL_HIST + 1 + L, C), F32)],
        compiler_params=pltpu.CompilerParams(dimension_semantics=("parallel",), vmem_limit_bytes=VMEM_LIMIT),
        name="pool",
    )(u, prev16, P['pool_w'], P['pool_sc'])


def _cmp_tokens(s16, gain, seg):
    n = s16.shape[0]
    nxt = jnp.where(lax.broadcasted_iota(I32, (n, 1), 0) == n - 1, 0.0, pltpu.roll(s16, n - 1, 0))
    cmp = (s16 + nxt) / float(CMP_BLOCK)
    kc = _seg_rms(cmp[:, 0:LANES], gain, seg)
    return kc, cmp[:, LANES:KV_ROW]


def _cmp_prompt_kernel(rows_ref, kg_ref, seg_ref, o_ref):
    L = rows_ref.shape[1]
    s16 = jnp.sum(rows_ref[0].reshape(L // CMP_STRIDE, CMP_STRIDE, KV_ROW), axis=1)
    kc, vc = _cmp_tokens(s16, kg_ref[0:1, :], seg_ref[...])
    o_ref[0, :, 0:LANES] = kc
    o_ref[0, :, LANES:KV_ROW] = vc


def _cmp_prompt(rows, layer, P):
    B, L, _ = rows.shape
    n16 = L // CMP_STRIDE
    return pl.pallas_call(
        _cmp_prompt_kernel,
        grid=(B,),
        in_specs=[pl.BlockSpec((1, L, KV_ROW), lambda b: (b, 0, 0)),
                  pl.BlockSpec((None, 3, LANES), lambda b: (layer, 0, 0)),
                  pl.BlockSpec((LANES, LANES), lambda b: (0, 0))],
        out_specs=pl.BlockSpec((1, n16, KV_ROW), lambda b: (b, 0, 0)),
        out_shape=jax.ShapeDtypeStruct((B, n16, KV_ROW), F32),
        compiler_params=pltpu.CompilerParams(dimension_semantics=("parallel",), vmem_limit_bytes=VMEM_LIMIT),
        name="cmp_prompt",
    )(rows, P['kg'], P['seg'])


def _page_sum_kernel(n_in, pt_ref, *refs):
    o_ref = refs[n_in]
    per = PAGE_SIZE // CMP_STRIDE
    for i in range(n_in):
        o_ref[0, 0, i * per:(i + 1) * per, :] = jnp.sum(
            refs[i][0].reshape(per, CMP_STRIDE, KV_ROW), axis=1)


def _page_sums(cache, page_table, pages_per_step):
    depth, n_phys = cache.shape[0], cache.shape[1]
    Bd, n_pages = page_table.shape
    flat = cache.reshape(depth * n_phys, PAGE_SIZE, KV_ROW)
    per = PAGE_SIZE // CMP_STRIDE
    n_in = pages_per_step

    def page_spec(i):
        return pl.BlockSpec((1, PAGE_SIZE, KV_ROW),
                            lambda l, b, p, pt: (l * n_phys + pt[b, p * n_in + i], 0, 0))

    return pl.pallas_call(
        functools.partial(_page_sum_kernel, n_in),
        grid_spec=pltpu.PrefetchScalarGridSpec(
            num_scalar_prefetch=1,
            grid=(depth, Bd, n_pages // n_in),
            in_specs=[page_spec(i) for i in range(n_in)],
            out_specs=pl.BlockSpec((1, 1, n_in * per, KV_ROW), lambda l, b, p, pt: (l, b, p, 0))),
        out_shape=jax.ShapeDtypeStruct((depth, Bd, n_pages * per, KV_ROW), F32),
        compiler_params=pltpu.CompilerParams(dimension_semantics=("parallel", "parallel", "arbitrary"),
                                             vmem_limit_bytes=VMEM_LIMIT),
        name="page_sums",
    )(page_table, *([flat] * n_in))


def _rank_select(score, lane, n_blocks, k_sel):
    cnt = jnp.zeros(score.shape, F32)
    for m in range(n_blocks):
        col = score[:, m:m + 1]
        beats = jnp.where(col > score, 1.0, jnp.where(col == score, jnp.where(lane > m, 1.0, 0.0), 0.0))
        cnt = cnt + beats
    return cnt


def _softmax_rows(s, mask):
    sm = jnp.where(mask, s, NEG)
    p = jnp.where(mask, jnp.exp(sm - jnp.max(sm, axis=-1, keepdims=True)), 0.0)
    den = jnp.sum(p, axis=-1, keepdims=True)
    return p / jnp.where(den > 0.0, den, 1.0)


def _nsa_prompt_kernel(TQ, KC, n_sel, q_ref, tail_ref, kcvc_ref, slc_ref, win_ref, ovl_ref, exp_ref, o_ref):
    t0 = pl.program_id(1) * TQ
    t = t0 + lax.broadcasted_iota(I32, (TQ, 1), 0)
    rep = lambda a: jnp.concatenate([a] * G_D, axis=0)
    t4 = rep(t)
    q = q_ref[0]
    gates = tail_ref[0]
    n_cmp = kcvc_ref.shape[1]
    lane_c = lax.broadcasted_iota(I32, (1, n_cmp), 1)
    cmask4 = (lane_c * CMP_STRIDE + (CMP_BLOCK - 1)) <= t4
    lane = lax.broadcasted_iota(I32, (1, LANES), 1)
    cur = t >> SEL_SHIFT
    n_chunks = (t0 + TQ + KC - 1) // KC
    k_sel = min(TOPK, n_sel)
    WK = WINDOW + TQ
    ws = pl.multiple_of(jnp.maximum(t0 - WINDOW, 0), TQ)
    dist = t4 - (ws + lax.broadcasted_iota(I32, (1, WK), 1))
    wmask = (dist >= 0) & (dist <= WINDOW)
    kvw = win_ref[0, pl.ds(ws, WK), :]

    for kvh in range(KVH_D):
        ksl = slice(kvh * HD_D, (kvh + 1) * HD_D)
        vsl = slice(LANES + kvh * HD_D, LANES + (kvh + 1) * HD_D)
        qs = (jnp.concatenate([q[:, (kvh * G_D + g) * HD_D:(kvh * G_D + g + 1) * HD_D] for g in range(G_D)],
                              axis=0) * (HD_D ** -0.5)).astype(BF16)
        pc = _softmax_rows(_dot_nt(qs, kcvc_ref[0, :, ksl].astype(BF16)), cmask4).astype(BF16)
        o_cmp = _dot(pc, kcvc_ref[0, :, vsl].astype(BF16))
        imp4 = _dot(pc, ovl_ref[...])
        imp = imp4[0:TQ]
        for g in range(1, G_D):
            imp = imp + imp4[g * TQ:(g + 1) * TQ]
        forced = (lane == 0) | (lane == cur) | (lane == cur - 1)
        score = jnp.where(lane <= cur, imp + jnp.where(forced, FORCE_BONUS, 0.0), -jnp.inf)
        cnt = _rank_select(score, lane, n_sel, k_sel)
        sel4 = rep(jnp.where(cnt < k_sel, 1.0, 0.0).astype(BF16))

        def body(j, carry, kvh=kvh, qs=qs, sel4=sel4, ksl=ksl, vsl=vsl):
            m_i, l_i, acc = carry
            kv = slc_ref[0, pl.ds(pl.multiple_of(j * KC, KC), KC), :]
            s = _dot_nt(qs, kv[:, ksl].astype(BF16))
            kpos = j * KC + lax.broadcasted_iota(I32, (1, KC), 1)
            msk = (_dot(sel4, exp_ref[j]) > 0.5) & (kpos <= t4)
            s = jnp.where(msk, s, NEG)
            m_new = jnp.maximum(m_i, jnp.max(s, axis=-1, keepdims=True))
            alpha = jnp.exp(m_i - m_new)
            p = jnp.where(msk, jnp.exp(s - m_new), 0.0)
            l_new = alpha * l_i + jnp.sum(p, axis=-1, keepdims=True)
            acc = alpha * acc + _dot(p.astype(BF16), kv[:, vsl].astype(BF16))
            return m_new, l_new, acc

        init = (jnp.full((G_D * TQ, 1), NEG, F32), jnp.zeros((G_D * TQ, 1), F32),
                jnp.zeros((G_D * TQ, HD_D), F32))
        _, l_f, acc_f = lax.fori_loop(0, n_chunks, body, init)
        o_slc = acc_f / l_f

        pw = _softmax_rows(_dot_nt(qs, kvw[:, ksl].astype(BF16)), wmask).astype(BF16)
        o_win = _dot(pw, kvw[:, vsl].astype(BF16))

        for g in range(G_D):
            hd = kvh * G_D + g
            rows = slice(g * TQ, (g + 1) * TQ)
            gc = TAIL_GATE0 + hd * 3
            o_ref[0, :, hd * HD_D:(hd + 1) * HD_D] = (gates[:, gc:gc + 1] * o_cmp[rows]
                                                     + gates[:, gc + 1:gc + 2] * o_slc[rows]
                                                     + gates[:, gc + 2:gc + 3] * o_win[rows])


def _nsa_prompt(nq, tail, kcvc, slc, win, C):
    B, L, _ = nq.shape
    TQ, KC = C['TQ'], C['KC']
    n_sel = L // SEL_BLOCK
    n16 = L // CMP_STRIDE
    qspec = lambda w: pl.BlockSpec((1, TQ, w), lambda b, i: (b, i, 0))
    full = lambda n: pl.BlockSpec((1, n, KV_ROW), lambda b, i: (b, 0, 0))
    return pl.pallas_call(
        functools.partial(_nsa_prompt_kernel, TQ, KC, n_sel),
        grid=(B, L // TQ),
        in_specs=[qspec(H_D * HD_D), qspec(LANES), full(n16), full(L), full(L),
                  pl.BlockSpec((n16, LANES), lambda b, i: (0, 0)),
                  pl.BlockSpec((L // KC, LANES, KC), lambda b, i: (0, 0, 0))],
        out_specs=qspec(H_D * HD_D),
        out_shape=jax.ShapeDtypeStruct((B, L, H_D * HD_D), F32),
        compiler_params=pltpu.CompilerParams(dimension_semantics=("parallel", "arbitrary"),
                                             vmem_limit_bytes=VMEM_LIMIT),
        name="nsa_prompt",
    )(nq, tail, kcvc, slc, win, C['ovl_p'], C['exp_p'])


def _stack_heads(qrow, kvh):
    return jnp.concatenate([qrow[:, (kvh * G_D + g) * HD_D:(kvh * G_D + g + 1) * HD_D] for g in range(G_D)],
                           axis=0) * (HD_D ** -0.5)


def _nsa_dec_select_kernel(t_now, n_sel, q_ref, s16_ref, kg_ref, seg_ref, ovl_ref, ocmp_ref, idx_ref):
    kc, vc = _cmp_tokens(s16_ref[0, 0], kg_ref[0:1, :], seg_ref[...])
    n_cmp = kc.shape[0]
    lane_c = lax.broadcasted_iota(I32, (G_D, n_cmp), 1)
    cmask = (lane_c * CMP_STRIDE + (CMP_BLOCK - 1)) <= t_now
    n_lane = ovl_ref.shape[1]
    lane = lax.broadcasted_iota(I32, (1, n_lane), 1)
    cur = t_now // SEL_BLOCK
    k_sel = min(TOPK, n_sel)
    q = q_ref[0]
    for kvh in range(KVH_D):
        qs = _stack_heads(q, kvh).astype(BF16)
        pc = _softmax_rows(_dot_nt(qs, kc[:, kvh * HD_D:(kvh + 1) * HD_D].astype(BF16)), cmask).astype(BF16)
        o_cmp = _dot(pc, vc[:, kvh * HD_D:(kvh + 1) * HD_D].astype(BF16))
        for g in range(G_D):
            hd = kvh * G_D + g
            ocmp_ref[0, :, hd * HD_D:(hd + 1) * HD_D] = o_cmp[g:g + 1, :]
        imp = jnp.sum(_dot(pc, ovl_ref[...]), axis=0, keepdims=True)
        forced = (lane == 0) | (lane == cur) | (lane == cur - 1)
        score = jnp.where(lane <= cur, imp + jnp.where(forced, FORCE_BONUS, 0.0), -jnp.inf)
        cnt = _rank_select(score, lane, n_sel, k_sel)
        lane_f = lane.astype(F32)
        row = jnp.zeros((1, LANES), F32)
        out_lane = lax.broadcasted_iota(I32, (1, LANES), 1)
        for r in range(k_sel):
            pick = jnp.sum(jnp.where(cnt == float(r), lane_f, 0.0), axis=1, keepdims=True)
            row = jnp.where(out_lane == r, pick, row)
        idx_ref[0, kvh:kvh + 1, :] = row.astype(I32)


def _nsa_dec_select(nq, s16_all, layer, P, C, t_now):
    Bd = nq.shape[0]
    n16 = s16_all.shape[2]
    n_sel = C['n_sel_d']
    n_lane = C['ovl_d'].shape[1]
    return pl.pallas_call(
        functools.partial(_nsa_dec_select_kernel, t_now, n_sel),
        grid=(Bd,),
        in_specs=[pl.BlockSpec((1, 1, H_D * HD_D), lambda b: (b, 0, 0)),
                  pl.BlockSpec((1, 1, n16, KV_ROW), lambda b: (layer, b, 0, 0)),
                  pl.BlockSpec((None, 3, LANES), lambda b: (layer, 0, 0)),
                  pl.BlockSpec((LANES, LANES), lambda b: (0, 0)),
                  pl.BlockSpec((n16, n_lane), lambda b: (0, 0))],
        out_specs=[pl.BlockSpec((1, 1, H_D * HD_D), lambda b: (b, 0, 0)),
                   pl.BlockSpec((1, KVH_D, LANES), lambda b: (b, 0, 0))],
        out_shape=[jax.ShapeDtypeStruct((Bd, 1, H_D * HD_D), F32),
                   jax.ShapeDtypeStruct((Bd, KVH_D, LANES), I32)],
        compiler_params=pltpu.CompilerParams(dimension_semantics=("parallel",), vmem_limit_bytes=VMEM_LIMIT),
        name="nsa_dec_select",
    )(nq.reshape(Bd, 1, H_D * HD_D), s16_all, P['kg'], P['seg'], C['ovl_d'])


def _attend_rows(qs, k_all, v_all, valid, k_new, v_new):
    s = _dot_nt(qs.astype(BF16), k_all.astype(BF16))
    if valid is not None:
        s = jnp.where(valid > 0, s, NEG)
    s_new = jnp.sum(qs * k_new, axis=1, keepdims=True)
    m = jnp.maximum(jnp.max(s, axis=-1, keepdims=True), s_new)
    p = jnp.exp(s - m)
    if valid is not None:
        p = jnp.where(valid > 0, p, 0.0)
    p_new = jnp.exp(s_new - m)
    den = jnp.sum(p, axis=-1, keepdims=True) + p_new
    return (_dot(p.astype(BF16), v_all.astype(BF16)) + p_new * v_new) / den


def _nsa_dec_attend_kernel(n_past_blocks, k_sel, pt_ref, ix_ref, q_ref, tail_ref, ocmp_ref, slcn_ref, winn_ref,
                           winc_ref, *refs):
    blk_refs = refs[:KVH_D * k_sel]
    o_ref = refs[KVH_D * k_sel]
    b = pl.program_id(0)
    q = q_ref[0]
    gates = tail_ref[0]
    ocmp = ocmp_ref[0]
    slc_new = slcn_ref[0]
    win_new = winn_ref[0]
    winc = winc_ref[0]
    lane_blk = lax.broadcasted_iota(I32, (G_D, k_sel * SEL_BLOCK), 1) >> SEL_SHIFT
    for kvh in range(KVH_D):
        ksl = slice(kvh * HD_D, (kvh + 1) * HD_D)
        vsl = slice(LANES + kvh * HD_D, LANES + (kvh + 1) * HD_D)
        qs = _stack_heads(q, kvh)
        blocks = [blk_refs[kvh * k_sel + r][0] for r in range(k_sel)]
        k_all = jnp.concatenate([x[:, ksl] for x in blocks], axis=0)
        v_all = jnp.concatenate([x[:, vsl] for x in blocks], axis=0)
        valid = jnp.zeros((G_D, k_sel * SEL_BLOCK), I32)
        for r in range(k_sel):
            in_past = (ix_ref[b, kvh * LANES + r] < n_past_blocks).astype(I32)
            valid = jnp.where(lane_blk == r, in_past, valid)
        o_slc = _attend_rows(qs, k_all, v_all, valid, slc_new[:, ksl], slc_new[:, vsl])
        o_win = _attend_rows(qs, winc[:, ksl], winc[:, vsl], None, win_new[:, ksl], win_new[:, vsl])
        for g in range(G_D):
            hd = kvh * G_D + g
            gc = TAIL_GATE0 + hd * 3
            hsl = slice(hd * HD_D, (hd + 1) * HD_D)
            o_ref[0, :, hsl] = (gates[:, gc:gc + 1] * ocmp[:, hsl] + gates[:, gc + 1:gc + 2] * o_slc[g:g + 1, :]
                                + gates[:, gc + 2:gc + 3] * o_win[g:g + 1, :])


def _nsa_dec_attend(nq, tail, ocmp, idx, slc_new, win_new, cache_slc, cache_win, page_table, layer, C):
    Bd = nq.shape[0]
    depth, n_phys = cache_slc.shape[0], cache_slc.shape[1]
    n_past_blocks = page_table.shape[1] * (PAGE_SIZE // SEL_BLOCK)
    k_sel = min(TOPK, C['n_sel_d'])
    per_page = PAGE_SIZE // SEL_BLOCK
    slc_flat = cache_slc.reshape(depth * n_phys * per_page, SEL_BLOCK, KV_ROW)
    n_win = cache_win.shape[2]
    win_flat = cache_win.reshape(depth * Bd, n_win, KV_ROW)

    def blk_spec(kvh, r):
        def imap(b, pt, ix):
            blk = jnp.minimum(ix[b, kvh * LANES + r], n_past_blocks - 1)
            return ((layer * n_phys + pt[b, blk // per_page]) * per_page + blk % per_page, 0, 0)
        return pl.BlockSpec((1, SEL_BLOCK, KV_ROW), imap)

    one = lambda w: pl.BlockSpec((1, 1, w), lambda b, pt, ix: (b, 0, 0))
    r3 = lambda a: a.reshape(Bd, 1, a.shape[-1])
    o = pl.pallas_call(
        functools.partial(_nsa_dec_attend_kernel, n_past_blocks, k_sel),
        grid_spec=pltpu.PrefetchScalarGridSpec(
            num_scalar_prefetch=2,
            grid=(Bd,),
            in_specs=[one(H_D * HD_D), one(LANES), one(H_D * HD_D), one(KV_ROW), one(KV_ROW),
                      pl.BlockSpec((1, n_win, KV_ROW), lambda b, pt, ix: (layer * Bd + b, 0, 0))]
                     + [blk_spec(kvh, r) for kvh in range(KVH_D) for r in range(k_sel)],
            out_specs=one(H_D * HD_D)),
        out_shape=jax.ShapeDtypeStruct((Bd, 1, H_D * HD_D), F32),
        compiler_params=pltpu.CompilerParams(dimension_semantics=("arbitrary",), vmem_limit_bytes=VMEM_LIMIT),
        name="nsa_dec_attend",
    )(page_table, idx.reshape(Bd, KVH_D * LANES), r3(nq), r3(tail), ocmp, r3(slc_new), r3(win_new), win_flat,
      *([slc_flat] * (KVH_D * k_sel)))
    return o.reshape(Bd, H_D * HD_D)


def _merge_kernel(x_ref, b0_ref, b1_ref, b2_ref, b3_ref, n1_ref, wg_ref, bg_ref, wb_ref, wo_ref, o_ref):
    x = x_ref[...]
    D = x.shape[1]
    hb = _rms(x, n1_ref[...]).astype(BF16)
    acc = jnp.zeros(x.shape, F32)
    for n, br in enumerate((b0_ref, b1_ref, b2_ref, b3_ref)):
        gate = _sigmoid(_dot(hb, wg_ref[:, n * D:(n + 1) * D]) + bg_ref[:, n * D:(n + 1) * D])
        acc = acc + gate * _dot(br[...].astype(BF16), wb_ref[n])
    o_ref[...] = x + _dot(acc.astype(BF16), wo_ref[...])


def _merge(x2d, branches, layer, P, tm):
    T, D = x2d.shape
    row = lambda w: pl.BlockSpec((tm, w), lambda i: (i, 0))
    lay2 = lambda a, b: pl.BlockSpec((None, a, b), lambda i: (layer, 0, 0))
    return pl.pallas_call(
        _merge_kernel,
        grid=(T // tm,),
        in_specs=[row(D)] + [row(b.shape[1]) for b in branches]
                 + [lay2(1, D), lay2(D, N_BRANCH * D), lay2(1, N_BRANCH * D),
                    pl.BlockSpec((None, N_BRANCH, D // 2, D), lambda i: (layer, 0, 0, 0)), lay2(D, D)],
        out_specs=row(D),
        out_shape=jax.ShapeDtypeStruct((T, D), F32),
        compiler_params=pltpu.CompilerParams(dimension_semantics=("parallel",), vmem_limit_bytes=VMEM_LIMIT),
        name="merge",
    )(x2d, *branches, P['n1'], P['w_gate'], P['b_gate'], P['w_branch'], P['w_out'])


def _mlp_kernel(x_ref, n2_ref, w1_ref, w2_ref, o_ref):
    x = x_ref[...]
    u = jnp.maximum(_dot(_rms(x, n2_ref[...]).astype(BF16), w1_ref[...]), 0.0)
    o_ref[...] = x + _dot((u * u).astype(BF16), w2_ref[...])


def _mlp(x2d, layer, P, tm):
    T, D = x2d.shape
    dff = P['w1'].shape[2]
    row = pl.BlockSpec((tm, D), lambda i: (i, 0))
    return pl.pallas_call(
        _mlp_kernel,
        grid=(T // tm,),
        in_specs=[row, pl.BlockSpec((None, 1, D), lambda i: (layer, 0, 0)),
                  pl.BlockSpec((None, D, dff), lambda i: (layer, 0, 0)),
                  pl.BlockSpec((None, dff, D), lambda i: (layer, 0, 0))],
        out_specs=row,
        out_shape=jax.ShapeDtypeStruct((T, D), F32),
        compiler_params=pltpu.CompilerParams(dimension_semantics=("parallel",), vmem_limit_bytes=VMEM_LIMIT),
        name="mlp",
    )(x2d, P['n2'], P['w1'], P['w2'])


def _prep_params(norm1_g, w_in, gla_wa2, gla_ba, gla_norm_g, hgrn_lb_param, hgrn_norm_g, pool_w, pool_scale,
                 nsa_q_norm_g, nsa_k_norm_g, w_branch, w_gate, b_gate, w_out, norm2_g, mlp_w1, mlp_w2):
    depth = w_in.shape[0]
    sizes = (QK, QK, VW, VW, GLA_RANK, QK, QK, VW, VW, 512, H_D * HD_D) + (KVH_D * HD_D,) * 6 + (H_D * 3,)
    offs = [0]
    for s in sizes:
        offs.append(offs[-1] + s)
    col = lambda i: w_in[:, :, offs[i]:offs[i + 1]]
    order = [0, 1, 2, 3, 5, 6, 7, 8, 9, 10, 11, 12, 13, 14, 15, 16, 4, 17]
    pad = jnp.zeros(w_in.shape[:2] + (LANES - GLA_RANK - H_D * 3,), w_in.dtype)
    w_in_r = jnp.concatenate([col(i) for i in order] + [pad], axis=-1).astype(BF16)
    wa2 = jnp.concatenate([gla_wa2, jnp.zeros((depth, LANES - GLA_RANK, QK), gla_wa2.dtype)], axis=1).astype(BF16)
    r3 = lambda a: a.reshape(depth, 1, a.shape[-1])
    lane = jnp.arange(LANES)
    seg = jnp.where((lane[:, None] // HD_D) == (lane[None, :] // HD_D), 1.0 / HD_D, 0.0).astype(BF16)
    return dict(
        n1=r3(norm1_g), w_in=w_in_r, wa2=wa2, ba=r3(gla_ba), lbp=hgrn_lb_param,
        qg=r3(jnp.tile(nsa_q_norm_g, (1, H_D))), kg=jnp.tile(nsa_k_norm_g, (1, 1, KVH_D)), seg=seg,
        gla_g=gla_norm_g, hgrn_g=hgrn_norm_g, pool_w=pool_w.astype(BF16), pool_sc=r3(pool_scale),
        w_gate=w_gate.astype(BF16), b_gate=r3(b_gate), w_branch=w_branch.astype(BF16), w_out=w_out.astype(BF16),
        n2=r3(norm2_g), w1=mlp_w1.astype(BF16), w2=mlp_w2.astype(BF16))


def _overlap(n_cmp_rows, n_lanes):
    per = SEL_BLOCK // CMP_STRIDE
    ci = jnp.arange(n_cmp_rows)[:, None]
    sj = jnp.arange(n_lanes)[None, :]
    return jnp.where((ci // per == sj) | ((ci + 1) // per == sj), 1.0, 0.0).astype(BF16)


def _prep_consts(L, past):
    TQ = 128
    KC = min(512, L)
    kpos = jnp.arange(L).reshape(L // KC, 1, KC)
    exp_p = jnp.where(kpos // SEL_BLOCK == jnp.arange(LANES)[None, :, None], 1.0, 0.0).astype(BF16)
    n16_d = past // CMP_STRIDE
    n_sel_d = (past + 1 + SEL_BLOCK - 1) // SEL_BLOCK
    n_lane_d = -(-n_sel_d // LANES) * LANES
    return dict(TQ=TQ, KC=KC, ovl_p=_overlap(L // CMP_STRIDE, LANES), exp_p=exp_p,
                ovl_d=_overlap(n16_d, n_lane_d), n_sel_d=n_sel_d)


def kernel(x_prompt, x_sample, state_gla, state_hgrn, state_pool, cache_cmp_kv, cache_slc_kv, cache_win_kv,
           page_table, norm1_g, w_in, gla_wa2, gla_ba, gla_norm_g, hgrn_lb_param, hgrn_norm_g, pool_w,
           pool_scale, nsa_q_norm_g, nsa_k_norm_g, w_branch, w_gate, b_gate, w_out, norm2_g, mlp_w1, mlp_w2):
    B, L, D = x_prompt.shape
    Bd = x_sample.shape[0]
    depth = w_in.shape[0]
    n_pages = page_table.shape[1]
    past = n_pages * PAGE_SIZE
    assert x_sample.shape[1] == 1 and L % 512 == 0 and L >= WINDOW + 128 and past % SEL_BLOCK == 0
    assert cache_win_kv.shape[2] == WINDOW and past >= WINDOW

    P = _prep_params(norm1_g, w_in, gla_wa2, gla_ba, gla_norm_g, hgrn_lb_param, hgrn_norm_g, pool_w, pool_scale,
                     nsa_q_norm_g, nsa_k_norm_g, w_branch, w_gate, b_gate, w_out, norm2_g, mlp_w1, mlp_w2)
    C = _prep_consts(L, past)
    tm_p = 256
    cb = 128
    bb = 8 if Bd % 8 == 0 else Bd
    keep = min(WINDOW, L)

    pps = 16 if n_pages % 16 == 0 else n_pages
    s16_all = _page_sums(cache_cmp_kv, page_table, pps)

    xp = x_prompt.reshape(B * L, D)
    xs = x_sample.reshape(Bd, D)
    outs = {k: [] for k in ('gla_p', 'gla_s', 'hgrn_p', 'hgrn_s', 'pool_p', 'pool_s', 'cmp_p', 'cmp_s',
                            'slc_p', 'slc_s', 'win_p', 'win_s')}
    zero_prev = jnp.zeros((B, POOL_HIST + 1, 512), F32)
    rows5 = lambda a, n: a.reshape(n, -1, 2, KVH_D, HD_D)
    for l in range(depth):
        gqkg, gvr, hqkg, hvr, u, nq, cmp_r, slc_r, win_r, tail = _in_proj(xp, l, P, tm_p)
        b3 = lambda a: a.reshape(B, L, a.shape[-1])
        o_gla, s_gla = _gla_prompt(b3(gqkg), b3(gvr), P['gla_g'][l:l + 1], cb)
        o_hgrn, s_hgrn = _gla_prompt(b3(hqkg), b3(hvr), P['hgrn_g'][l:l + 1], cb)
        o_pool = _pool(b3(u), zero_prev, l, P, 0)
        kcvc = _cmp_prompt(b3(cmp_r), l, P)
        o_nsa = _nsa_prompt(b3(nq), b3(tail), kcvc, b3(slc_r), b3(win_r), C)
        flat = lambda a: a.reshape(B * L, a.shape[-1])
        xp = _merge(xp, (flat(o_gla), flat(o_hgrn), flat(o_pool), flat(o_nsa)), l, P, tm_p)
        xp = _mlp(xp, l, P, tm_p)
        outs['gla_p'].append(s_gla)
        outs['hgrn_p'].append(s_hgrn)
        outs['pool_p'].append(b3(u)[:, L - POOL_HIST:])
        outs['cmp_p'].append(rows5(cmp_r, B))
        outs['slc_p'].append(rows5(slc_r, B))
        outs['win_p'].append(rows5(win_r, B)[:, L - keep:])

        gqkg, gvr, hqkg, hvr, u, nq, cmp_r, slc_r, win_r, tail = _in_proj(xs, l, P, Bd)
        o_gla, s_gla = _gla_decode(gqkg, gvr, P['gla_g'][l], state_gla, l, bb)
        o_hgrn, s_hgrn = _gla_decode(hqkg, hvr, P['hgrn_g'][l], state_hgrn, l, bb)
        prev = state_pool[l]
        prev16 = jnp.concatenate([jnp.zeros((Bd, 1, 512), F32), prev], axis=1)
        o_pool = _pool(u.reshape(Bd, 1, 512), prev16, l, P, past).reshape(Bd, 512)
        ocmp, idx = _nsa_dec_select(nq, s16_all, l, P, C, past)
        o_nsa = _nsa_dec_attend(nq, tail, ocmp, idx, slc_r, win_r, cache_slc_kv, cache_win_kv, page_table, l, C)
        xs = _merge(xs, (o_gla, o_hgrn, o_pool, o_nsa), l, P, Bd)
        xs = _mlp(xs, l, P, Bd)
        outs['gla_s'].append(s_gla)
        outs['hgrn_s'].append(s_hgrn)
        outs['pool_s'].append(jnp.concatenate([prev[:, 1:], u.reshape(Bd, 1, 512)], axis=1))
        outs['cmp_s'].append(rows5(cmp_r, Bd))
        outs['slc_s'].append(rows5(slc_r, Bd))
        outs['win_s'].append(rows5(win_r, Bd))

    st = lambda k: jnp.stack(outs[k])
    return (xp.reshape(B, L, D), xs.reshape(Bd, 1, D),
            st('gla_p'), st('gla_s'), st('hgrn_p'), st('hgrn_s'), st('pool_p'), st('pool_s'),
            st('cmp_p'), st('cmp_s'), st('slc_p'), st('slc_s'), st('win_p'), st('win_s'))
```

```python
import functools

import jax
import jax.numpy as jnp
from jax import lax
from jax.experimental import pallas as pl
from jax.experimental.pallas import tpu as pltpu

F32 = jnp.float32
BF16 = jnp.bfloat16
I32 = jnp.int32

N_BRANCH = 4
H_LIN = 4
DK_LIN = 64
DV_LIN = 128
GLA_RANK = 16
GLA_TAU = 16.0
POOL_WINDOWS = (2, 4, 8, 16)
POOL_HIST = 15
POOL_GROUP = 128
H_D = 8
HD_D = 64
KVH_D = 2
G_D = 4
CMP_STRIDE = 16
CMP_BLOCK = 32
SEL_BLOCK = 64
SEL_SHIFT = 6
TOPK = 16
WINDOW = 512
FORCE_BONUS = 1000.0
PAGE_SIZE = 128
PAGE_SHIFT = 7
EPS = 1e-6
NEG = -1e30
LOG2E = 1.4426950408889634

LANES = 128
VMEM_LIMIT = 56 * 1024 * 1024

QK = H_LIN * DK_LIN
VW = H_LIN * DV_LIN
OFF_GLA = 0
OFF_HGRN = OFF_GLA + 2 * QK + 2 * VW
OFF_POOL = OFF_HGRN + 2 * QK + 2 * VW
OFF_NQ = OFF_POOL + 512
OFF_KV = OFF_NQ + H_D * HD_D
OFF_TAIL = OFF_KV + 6 * KVH_D * HD_D
IN_PAD = OFF_TAIL + LANES
TAIL_GATE0 = GLA_RANK
KV_ROW = 2 * KVH_D * HD_D


def _dot(a, b):
    return jnp.dot(a, b, preferred_element_type=F32)


def _dot_nt(a, b):
    return lax.dot_general(a, b, (((1,), (1,)), ((), ())), preferred_element_type=F32)


def _rms(x, gain):
    return x * lax.rsqrt(jnp.mean(x * x, axis=-1, keepdims=True) + EPS) * gain


def _seg_mean(x2, seg):
    parts = []
    for c in range(x2.shape[1] // LANES):
        xc = x2[:, c * LANES:(c + 1) * LANES]
        h1 = xc.astype(BF16)
        r1 = xc - h1.astype(F32)
        h2 = r1.astype(BF16)
        h3 = (r1 - h2.astype(F32)).astype(BF16)
        parts.append(_dot(h1, seg) + _dot(h2, seg) + _dot(h3, seg))
    return parts[0] if len(parts) == 1 else jnp.concatenate(parts, axis=1)


def _seg_rms(x, gain, seg):
    return x * lax.rsqrt(_seg_mean(x * x, seg) + EPS) * gain


def _sigmoid(x):
    return 1.0 / (1.0 + jnp.exp(-x))


def _in_proj_kernel(layer, kv_t, x_ref, n1_ref, w_ref, wa2_ref, ba_ref, lbp_ref, qg_ref, kg_ref, seg_ref,
                    wkvt_ref, kgc_ref,
                    gqkg_ref, gvr_ref, hqkg_ref, hvr_ref, u_ref, nq_ref, cmp_ref, slc_ref, win_ref, tail_ref):
    x = x_ref[...]
    hb = _rms(x, n1_ref[...]).astype(BF16)
    seg = seg_ref[...]

    def proj(c0, c1):
        return _dot(hb, w_ref[:, c0:c1])

    tail = proj(OFF_TAIL, IN_PAD)
    tail_ref[...] = _sigmoid(tail)

    a_pre = _dot(tail.astype(BF16), wa2_ref[...]) + ba_ref[...]
    log_a = (jnp.minimum(a_pre, 0.0) - jnp.log(1.0 + jnp.exp(-jnp.abs(a_pre)))) / GLA_TAU
    z = proj(OFF_GLA, OFF_GLA + 2 * QK)
    gqkg_ref[:, 0:QK] = z[:, 0:QK] * (DK_LIN ** -0.5)
    gqkg_ref[:, QK:2 * QK] = z[:, QK:2 * QK]
    gqkg_ref[:, 2 * QK:3 * QK] = log_a
    gvr_ref[...] = proj(OFF_GLA + 2 * QK, OFF_HGRN)

    lbp = lbp_ref[...]
    pe = jnp.exp(lbp - jnp.max(lbp, axis=0, keepdims=True))
    soft = pe / jnp.sum(pe, axis=0, keepdims=True)
    lb = jnp.zeros((1, QK), F32)
    for i in range(1, layer + 1):
        lb = lb + soft[i:i + 1, :]
    z = proj(OFF_HGRN, OFF_HGRN + 2 * QK)
    hq = z[:, 0:QK]
    fgate = lb + (1.0 - lb) * _sigmoid(z[:, QK:2 * QK])
    hqkg_ref[:, 0:QK] = hq * _sigmoid(hq) * (DK_LIN ** -0.5)
    hqkg_ref[:, QK:2 * QK] = 1.0 - fgate
    hqkg_ref[:, 2 * QK:3 * QK] = jnp.log(fgate)
    hvr_ref[...] = proj(OFF_HGRN + 2 * QK, OFF_POOL)

    u_ref[...] = proj(OFF_POOL, OFF_NQ)
    nq_ref[...] = _seg_rms(proj(OFF_NQ, OFF_KV), qg_ref[...], seg)

    if kv_t:
        zt = _dot_nt(wkvt_ref[...], hb)
        cmp_ref[0] = zt[0:KV_ROW]

        def normed_rows(base, gi):
            ks = [_norm_feature_major(zt[base + kvh * HD_D:base + (kvh + 1) * HD_D], kgc_ref[gi])
                  for kvh in range(KVH_D)]
            return jnp.concatenate(ks + [zt[base + LANES:base + KV_ROW]], axis=0)

        slc_ref[0, 0] = normed_rows(KV_ROW, 1)
        win = normed_rows(2 * KV_ROW, 2)
        for c in range(win_ref.shape[1]):
            win_ref[0, c] = win[:, c * LANES:(c + 1) * LANES]
    else:
        zkv = proj(OFF_KV, OFF_TAIL)
        cmp_ref[...] = zkv[:, 0:KV_ROW]
        slc_ref[:, 0:LANES] = _seg_rms(zkv[:, 256:384], kg_ref[1:2, :], seg)
        slc_ref[:, LANES:KV_ROW] = zkv[:, 384:512]
        win_ref[:, 0:LANES] = _seg_rms(zkv[:, 512:640], kg_ref[2:3, :], seg)
        win_ref[:, LANES:KV_ROW] = zkv[:, 640:768]


def _in_proj(x2d, layer, P, tm, seq_len=None, kc=None):
    T, D = x2d.shape
    depth = P['lbp'].shape[0]
    kv_t = seq_len is not None
    row = lambda w: pl.BlockSpec((tm, w), lambda i: (i, 0))
    lay2 = lambda a, b: pl.BlockSpec((None, a, b), lambda i: (layer, 0, 0))
    widths = (3 * QK, 2 * VW, 3 * QK, 2 * VW, 512, H_D * HD_D)
    out_specs = [row(w) for w in widths]
    out_shape = [jax.ShapeDtypeStruct((T, w), F32) for w in widths]
    if kv_t:
        nt = seq_len // tm
        per = kc // tm
        nb = T // seq_len
        out_specs += [pl.BlockSpec((1, KV_ROW, tm), lambda i: (i // nt, 0, i % nt)),
                      pl.BlockSpec((1, 1, KV_ROW, tm), lambda i: (i // nt, (i % nt) // per, 0, (i % nt) % per)),
                      pl.BlockSpec((1, tm // LANES, KV_ROW, LANES), lambda i: (i // nt, i % nt, 0, 0))]
        out_shape += [jax.ShapeDtypeStruct((nb, KV_ROW, seq_len), F32),
                      jax.ShapeDtypeStruct((nb, seq_len // kc, KV_ROW, kc), F32),
                      jax.ShapeDtypeStruct((nb, seq_len // LANES, KV_ROW, LANES), F32)]
    else:
        out_specs += [row(KV_ROW)] * 3
        out_shape += [jax.ShapeDtypeStruct((T, KV_ROW), F32)] * 3
    out_specs.append(row(LANES))
    out_shape.append(jax.ShapeDtypeStruct((T, LANES), F32))
    return pl.pallas_call(
        functools.partial(_in_proj_kernel, layer, kv_t),
        grid=(T // tm,),
        in_specs=[row(D), lay2(1, D), lay2(D, IN_PAD), lay2(LANES, QK), lay2(1, QK),
                  pl.BlockSpec((depth, QK), lambda i: (0, 0)), lay2(1, H_D * HD_D), lay2(3, LANES),
                  pl.BlockSpec((LANES, LANES), lambda i: (0, 0)), lay2(3 * KV_ROW, D),
                  pl.BlockSpec((None, 3, HD_D, 1), lambda i: (layer, 0, 0, 0))],
        out_specs=out_specs,
        out_shape=out_shape,
        compiler_params=pltpu.CompilerParams(dimension_semantics=("parallel",), vmem_limit_bytes=VMEM_LIMIT),
        name="in_proj",
    )(x2d, P['n1'], P['w_in'], P['wa2'], P['ba'], P['lbp'], P['qg'], P['kg'], P['seg'], P['w_kvt'], P['kgc'])


def _gla_kernel(levels, qkg_ref, vr_ref, gain_ref, o_ref, s_ref):
    CB = qkg_ref.shape[1]

    @pl.when(pl.program_id(1) == 0)
    def _():
        s_ref[...] = jnp.zeros_like(s_ref)

    q = qkg_ref[0, :, 0:QK]
    k = qkg_ref[0, :, QK:2 * QK]
    g = qkg_ref[0, :, 2 * QK:3 * QK]
    row = lax.broadcasted_iota(I32, (CB, 1), 0)
    r2 = lax.broadcasted_iota(I32, (CB, CB), 0)
    c2 = lax.broadcasted_iota(I32, (CB, CB), 1)
    heads = [slice(h * DK_LIN, (h + 1) * DK_LIN) for h in range(H_LIN)]

    qb = q.astype(BF16)
    kb = k.astype(BF16)
    att = [jnp.where(r2 == c2, _dot_nt(qb[:, hs], kb[:, hs]), 0.0) for hs in heads]
    c = g
    tot = g
    for lvl in range(levels):
        half = 1 << lvl
        upper = ((row >> lvl) & 1) == 1
        qt = jnp.where(upper, q * jnp.exp(c), 0.0).astype(BF16)
        kt = jnp.where(upper, 0.0, k * jnp.exp(tot - c)).astype(BF16)
        same = (r2 >> (lvl + 1)) == (c2 >> (lvl + 1))
        for h, hs in enumerate(heads):
            att[h] = att[h] + jnp.where(same, _dot_nt(qt[:, hs], kt[:, hs]), 0.0)
        t_lo = pltpu.roll(tot, half, 0)
        t_hi = pltpu.roll(tot, CB - half, 0)
        c = c + jnp.where(upper, t_lo, 0.0)
        tot = tot + jnp.where(upper, t_lo, t_hi)

    qd = (q * jnp.exp(c)).astype(BF16)
    kdt = jnp.transpose(k * jnp.exp(tot - c)).astype(BF16)
    eye = (lax.broadcasted_iota(I32, (DK_LIN, DK_LIN), 0) == lax.broadcasted_iota(I32, (DK_LIN, DK_LIN), 1))
    gain = gain_ref[...]
    for h, hs in enumerate(heads):
        vs = slice(h * DV_LIN, (h + 1) * DV_LIN)
        v = vr_ref[0, :, vs].astype(BF16)
        s_old = s_ref[0, h]
        o = _dot(att[h].astype(BF16), v) + _dot(qd[:, hs], s_old.astype(BF16))
        t_row = jnp.broadcast_to(tot[0:1, hs], (DK_LIN, DK_LIN))
        t_col = jnp.sum(jnp.where(eye, t_row, 0.0), axis=1, keepdims=True)
        s_ref[0, h] = s_old * jnp.exp(t_col) + _dot(kdt[hs, :], v)
        r = vr_ref[0, :, VW + h * DV_LIN:VW + (h + 1) * DV_LIN]
        o_ref[0, :, vs] = _rms(o, gain) * (r * _sigmoid(r))


def _gla_prompt(qkg, vr, gain, cb):
    B, L, _ = qkg.shape
    levels = cb.bit_length() - 1
    return pl.pallas_call(
        functools.partial(_gla_kernel, levels),
        grid=(B, L // cb),
        in_specs=[pl.BlockSpec((1, cb, 3 * QK), lambda b, j: (b, j, 0)),
                  pl.BlockSpec((1, cb, 2 * VW), lambda b, j: (b, j, 0)),
                  pl.BlockSpec((1, DV_LIN), lambda b, j: (0, 0))],
        out_specs=[pl.BlockSpec((1, cb, VW), lambda b, j: (b, j, 0)),
                   pl.BlockSpec((1, H_LIN, DK_LIN, DV_LIN), lambda b, j: (b, 0, 0, 0))],
        out_shape=[jax.ShapeDtypeStruct((B, L, VW), F32),
                   jax.ShapeDtypeStruct((B, H_LIN, DK_LIN, DV_LIN), F32)],
        compiler_params=pltpu.CompilerParams(dimension_semantics=("parallel", "arbitrary"),
                                             vmem_limit_bytes=VMEM_LIMIT),
        name="gla_prompt",
    )(qkg, vr, gain)


def _gla_dec_kernel(q_ref, k_ref, g_ref, v_ref, r_ref, gain_ref, s0_ref, o_ref, s_ref):
    s_new = s0_ref[...] * jnp.exp(g_ref[...]) + k_ref[...] * v_ref[...]
    s_ref[...] = s_new
    o = jnp.sum(q_ref[...] * s_new, axis=2, keepdims=True)
    r = r_ref[...]
    o_ref[...] = _rms(o, gain_ref[...]) * (r * _sigmoid(r))


def _gla_decode(qkg, vr, gain, state, layer, bb):
    Bd = qkg.shape[0]
    col = lambda a: a.reshape(Bd, H_LIN, DK_LIN, 1)
    rowv = lambda a: a.reshape(Bd, H_LIN, 1, DV_LIN)
    q, k, g = col(qkg[:, 0:QK]), col(qkg[:, QK:2 * QK]), col(qkg[:, 2 * QK:])
    v, r = rowv(vr[:, 0:VW]), rowv(vr[:, VW:])
    cspec = pl.BlockSpec((bb, H_LIN, DK_LIN, 1), lambda i: (i, 0, 0, 0))
    rspec = pl.BlockSpec((bb, H_LIN, 1, DV_LIN), lambda i: (i, 0, 0, 0))
    o, s_new = pl.pallas_call(
        _gla_dec_kernel,
        grid=(Bd // bb,),
        in_specs=[cspec, cspec, cspec, rspec, rspec,
                  pl.BlockSpec((1, 1, 1, DV_LIN), lambda i: (0, 0, 0, 0)),
                  pl.BlockSpec((None, bb, H_LIN, DK_LIN, DV_LIN), lambda i: (layer, i, 0, 0, 0))],
        out_specs=[rspec, pl.BlockSpec((bb, H_LIN, DK_LIN, DV_LIN), lambda i: (i, 0, 0, 0))],
        out_shape=[jax.ShapeDtypeStruct((Bd, H_LIN, 1, DV_LIN), F32),
                   jax.ShapeDtypeStruct((Bd, H_LIN, DK_LIN, DV_LIN), F32)],
        compiler_params=pltpu.CompilerParams(dimension_semantics=("parallel",), vmem_limit_bytes=VMEM_LIMIT),
        name="gla_decode",
    )(q, k, g, v, r, gain.reshape(1, 1, 1, DV_LIN), state)
    return o.reshape(Bd, VW), s_new


def _pool_kernel(pos0, rt, u_ref, prev_ref, w_ref, sc_ref, o_ref, ext_ref):
    L = u_ref.shape[1]
    hdr = POOL_HIST + 1
    ext_ref[0:hdr, :] = prev_ref[0]
    ext_ref[hdr:hdr + L, :] = u_ref[0]
    for r0 in range(0, L, rt):
        n_avail = (pos0 + r0 + 1 + lax.broadcasted_iota(I32, (rt, 1), 0)).astype(F32)
        for gi, w in enumerate(POOL_WINDOWS):
            cs = slice(gi * POOL_GROUP, (gi + 1) * POOL_GROUP)
            tok = ext_ref[hdr + r0:hdr + r0 + rt, cs]
            acc = tok
            for j in range(1, w):
                acc = acc + ext_ref[hdr + r0 - j:hdr + r0 - j + rt, cs]
            d = acc / jnp.minimum(float(w), n_avail) - tok
            o_ref[0, r0:r0 + rt, cs] = _dot(d.astype(BF16), w_ref[gi]) * sc_ref[:, cs]


def _pool(u, prev16, layer, P, pos0):
    B, L, C = u.shape
    rt = min(L, 256)
    return pl.pallas_call(
        functools.partial(_pool_kernel, pos0, rt),
        grid=(B,),
        in_specs=[pl.BlockSpec((1, L, C), lambda b: (b, 0, 0)),
                  pl.BlockSpec((1, POOL_HIST + 1, C), lambda b: (b, 0, 0)),
                  pl.BlockSpec((None, len(POOL_WINDOWS), POOL_GROUP, POOL_GROUP), lambda b: (layer, 0, 0, 0)),
                  pl.BlockSpec((None, 1, C), lambda b: (layer, 0, 0))],
        out_specs=pl.BlockSpec((1, L, C), lambda b: (b, 0, 0)),
        out_shape=jax.ShapeDtypeStruct((B, L, C), F32),
        scratch_shapes=[pltpu.VMEM((POOL_HIST + 1 + L, C), F32)],
        compiler_params=pltpu.CompilerParams(dimension_semantics=("parallel",), vmem_limit_bytes=VMEM_LIMIT),
        name="pool",
    )(u, prev16, P['pool_w'], P['pool_sc'])


def _pool_lanes(x, pm):
    hi = x.astype(BF16)
    lo = (x - hi.astype(F32)).astype(BF16)
    return _dot(hi, pm) + _dot(lo, pm)


def _norm_feature_major(x, gain_col):
    return x * lax.rsqrt(jnp.mean(x * x, axis=0, keepdims=True) + EPS) * gain_col


def _cmp_store(cmp, gain_col, o_ref):
    for kvh in range(KVH_D):
        rows = slice(kvh * HD_D, (kvh + 1) * HD_D)
        o_ref[0, rows, :] = _norm_feature_major(cmp[rows], gain_col)
    o_ref[0, LANES:KV_ROW, :] = cmp[LANES:KV_ROW]


def _cmp_prompt_kernel(rows_ref, kgc_ref, pm_ref, o_ref):
    _cmp_store(_pool_lanes(rows_ref[0], pm_ref[...]) / float(CMP_BLOCK), kgc_ref[0], o_ref)


def _cmp_prompt(rows_t, layer, P, C):
    B, _, L = rows_t.shape
    n16 = L // CMP_STRIDE
    return pl.pallas_call(
        _cmp_prompt_kernel,
        grid=(B,),
        in_specs=[pl.BlockSpec((1, KV_ROW, L), lambda b: (b, 0, 0)),
                  pl.BlockSpec((None, 3, HD_D, 1), lambda b: (layer, 0, 0, 0)),
                  pl.BlockSpec((L, n16), lambda b: (0, 0))],
        out_specs=pl.BlockSpec((1, KV_ROW, n16), lambda b: (b, 0, 0)),
        out_shape=jax.ShapeDtypeStruct((B, KV_ROW, n16), F32),
        compiler_params=pltpu.CompilerParams(dimension_semantics=("parallel",), vmem_limit_bytes=VMEM_LIMIT),
        name="cmp_prompt",
    )(rows_t, P['kgc'], C['pool32_p'])


def _feature_major(cache):
    nd = cache.ndim
    return jnp.transpose(cache, tuple(range(nd - 4)) + (nd - 3, nd - 2, nd - 1, nd - 4))


def _page_sum_kernel(n_in, pt_ref, *refs):
    pm_ref, o_ref = refs[n_in], refs[n_in + 1]
    x = jnp.concatenate([refs[i][0] for i in range(n_in)], axis=1)
    o_ref[0, 0] = _pool_lanes(x, pm_ref[...])


def _page_sums(cache, page_table, pages_per_step):
    depth, n_phys = cache.shape[0], cache.shape[1]
    Bd, n_pages = page_table.shape
    flat = cache.reshape(depth * n_phys, KV_ROW, PAGE_SIZE)
    per = PAGE_SIZE // CMP_STRIDE
    n_in = pages_per_step
    pos = jnp.arange(n_in * PAGE_SIZE)
    pm = jnp.where(pos[:, None] // CMP_STRIDE == jnp.arange(n_in * per)[None, :], 1.0, 0.0).astype(BF16)

    def page_spec(i):
        return pl.BlockSpec((1, KV_ROW, PAGE_SIZE),
                            lambda l, b, p, pt: (l * n_phys + pt[b, p * n_in + i], 0, 0))

    return pl.pallas_call(
        functools.partial(_page_sum_kernel, n_in),
        grid_spec=pltpu.PrefetchScalarGridSpec(
            num_scalar_prefetch=1,
            grid=(depth, Bd, n_pages // n_in),
            in_specs=[page_spec(i) for i in range(n_in)]
                     + [pl.BlockSpec((n_in * PAGE_SIZE, n_in * per), lambda l, b, p, pt: (0, 0))],
            out_specs=pl.BlockSpec((1, 1, KV_ROW, n_in * per), lambda l, b, p, pt: (l, b, 0, p))),
        out_shape=jax.ShapeDtypeStruct((depth, Bd, KV_ROW, n_pages * per), F32),
        compiler_params=pltpu.CompilerParams(dimension_semantics=("parallel", "parallel", "arbitrary"),
                                             vmem_limit_bytes=VMEM_LIMIT),
        name="page_sums",
    )(page_table, *([flat] * n_in), pm)


def _rank_count(score, idx, n_blocks, axis):
    cnt = jnp.zeros(score.shape, F32)
    for m in range(n_blocks):
        other = score[m:m + 1, :] if axis == 0 else score[:, m:m + 1]
        beats = jnp.where(other > score, 1.0, jnp.where(other == score, jnp.where(idx > m, 1.0, 0.0), 0.0))
        cnt = cnt + beats
    return cnt


def _softmax2_rows(s2, mask):
    sm = jnp.where(mask, s2, NEG)
    p = jnp.where(mask, jnp.exp2(sm - jnp.max(sm, axis=-1, keepdims=True)), 0.0)
    den = jnp.sum(p, axis=-1, keepdims=True)
    return p / jnp.where(den > 0.0, den, 1.0)


def _nsa_prompt_kernel(TQ, KC, n_sel, q_ref, tail_ref, kcvc_ref, slc_ref, win_ref, ovl_ref, exp_ref, o_ref):
    t0 = pl.program_id(1) * TQ
    R = G_D * TQ
    t = t0 + lax.broadcasted_iota(I32, (TQ, 1), 0)
    t4 = t0 + (lax.broadcasted_iota(I32, (R, 1), 0) & (TQ - 1))
    rep = lambda a: jnp.concatenate([a] * G_D, axis=0)
    q = q_ref[0]
    gates = tail_ref[0]
    n_cmp = kcvc_ref.shape[2]
    lane_c = lax.broadcasted_iota(I32, (1, n_cmp), 1)
    cmask4 = (lane_c * CMP_STRIDE + (CMP_BLOCK - 1)) <= t4
    blk = lax.broadcasted_iota(I32, (n_sel, 1), 0)
    cur = (t0 + lax.broadcasted_iota(I32, (1, TQ), 1)) >> SEL_SHIFT
    j_last = t0 // KC
    k_sel = min(TOPK, n_sel)
    add_rows = lambda s, bias: (s.reshape(G_D, TQ, s.shape[1]) + bias[None]).reshape(R, s.shape[1])
    lane_k = lax.broadcasted_iota(I32, (1, KC), 1)
    WK = WINDOW + TQ
    ws = jnp.maximum(t0 - WINDOW, 0)
    wc = ws // LANES
    dist = t - (ws + lax.broadcasted_iota(I32, (1, WK), 1))
    bias_w = jnp.where((dist >= 0) & (dist <= WINDOW), 0.0, NEG)

    for kvh in range(KVH_D):
        ksl = slice(kvh * HD_D, (kvh + 1) * HD_D)
        vsl = slice(LANES + kvh * HD_D, LANES + (kvh + 1) * HD_D)
        qs = (jnp.concatenate([q[:, (kvh * G_D + g) * HD_D:(kvh * G_D + g + 1) * HD_D] for g in range(G_D)],
                              axis=0) * (HD_D ** -0.5 * LOG2E)).astype(BF16)
        pcf = _softmax2_rows(_dot(qs, kcvc_ref[0, ksl, :].astype(BF16)), cmask4)
        o_cmp = _dot_nt(pcf.astype(BF16), kcvc_ref[0, vsl, :].astype(BF16))
        pcs = pcf[0:TQ]
        for g in range(1, G_D):
            pcs = pcs + pcf[g * TQ:(g + 1) * TQ]
        imp = _dot_nt(ovl_ref[...], pcs.astype(BF16))
        forced = (blk == 0) | (blk == cur) | (blk == cur - 1)
        score = jnp.where(blk <= cur, imp + jnp.where(forced, FORCE_BONUS, 0.0), -jnp.inf)
        cnt = _rank_count(score, blk, n_sel, 0)
        selb = jnp.where(cnt < k_sel, 0.0, NEG)
        selb = jnp.transpose(jnp.concatenate([selb, jnp.zeros((LANES - n_sel, TQ), F32)], axis=0))
        q_aug = jnp.concatenate([rep(selb.astype(BF16)), qs], axis=1)

        def chunk(j, carry, q_aug=q_aug, ksl=ksl, vsl=vsl):
            m_i, l_i, acc = carry
            k_aug = jnp.concatenate([exp_ref[j], slc_ref[0, j, ksl, :].astype(BF16)], axis=0)
            causal = jnp.where(j * KC + lane_k <= t, 0.0, NEG)
            s = add_rows(_dot(q_aug, k_aug), causal)
            m_new = jnp.maximum(m_i, jnp.max(s, axis=-1, keepdims=True))
            alpha = jnp.exp2(m_i - m_new)
            p = jnp.exp2(s - m_new)
            l_new = alpha * l_i + jnp.sum(p, axis=-1, keepdims=True)
            acc = alpha * acc + _dot_nt(p.astype(BF16), slc_ref[0, j, vsl, :].astype(BF16))
            return m_new, l_new, acc

        init = (jnp.full((R, 1), NEG, F32), jnp.zeros((R, 1), F32), jnp.zeros((R, HD_D), F32))
        _, l_f, acc_f = lax.fori_loop(0, j_last + 1, chunk, init)
        o_slc = acc_f / l_f

        win_rows = lambda sl: jnp.concatenate(
            [win_ref[0, wc + i, sl, :].astype(BF16) for i in range(WK // LANES)], axis=1)
        sw = add_rows(_dot(qs, win_rows(ksl)), bias_w)
        pw = jnp.exp2(sw - jnp.max(sw, axis=-1, keepdims=True))
        o_win = _dot_nt(pw.astype(BF16), win_rows(vsl)) / jnp.sum(pw, axis=-1, keepdims=True)

        for g in range(G_D):
            hd = kvh * G_D + g
            rows = slice(g * TQ, (g + 1) * TQ)
            gc = TAIL_GATE0 + hd * 3
            o_ref[0, :, hd * HD_D:(hd + 1) * HD_D] = (gates[:, gc:gc + 1] * o_cmp[rows]
                                                     + gates[:, gc + 1:gc + 2] * o_slc[rows]
                                                     + gates[:, gc + 2:gc + 3] * o_win[rows])


def _nsa_prompt(nq, tail, kcvc, slc, win, C):
    B, L, _ = nq.shape
    TQ, KC = C['TQ'], C['KC']
    n_sel = L // SEL_BLOCK
    n16 = L // CMP_STRIDE
    qspec = lambda w: pl.BlockSpec((1, TQ, w), lambda b, i: (b, i, 0))
    full = lambda n: pl.BlockSpec((1, KV_ROW, n), lambda b, i: (b, 0, 0))
    return pl.pallas_call(
        functools.partial(_nsa_prompt_kernel, TQ, KC, n_sel),
        grid=(B, L // TQ),
        in_specs=[qspec(H_D * HD_D), qspec(LANES), full(n16),
                  pl.BlockSpec((1, L // KC, KV_ROW, KC), lambda b, i: (b, 0, 0, 0)),
                  pl.BlockSpec((1, L // LANES, KV_ROW, LANES), lambda b, i: (b, 0, 0, 0)),
                  pl.BlockSpec((n_sel, n16), lambda b, i: (0, 0)),
                  pl.BlockSpec((L // KC, LANES, KC), lambda b, i: (0, 0, 0))],
        out_specs=qspec(H_D * HD_D),
        out_shape=jax.ShapeDtypeStruct((B, L, H_D * HD_D), F32),
        compiler_params=pltpu.CompilerParams(dimension_semantics=("parallel", "arbitrary"),
                                             vmem_limit_bytes=VMEM_LIMIT),
        name="nsa_prompt",
    )(nq, tail, kcvc, slc, win, C['ovl_p'], C['exp_p'])


def _stack_heads(qrow, kvh):
    return jnp.concatenate([qrow[:, (kvh * G_D + g) * HD_D:(kvh * G_D + g + 1) * HD_D] for g in range(G_D)],
                           axis=0) * (HD_D ** -0.5 * LOG2E)


def _nsa_dec_select_kernel(t_now, n_sel, q_ref, s16_ref, kgc_ref, ovl_ref, ocmp_ref, idx_ref):
    s16 = s16_ref[0, 0]
    n_cmp = s16.shape[1]
    nxt = jnp.where(lax.broadcasted_iota(I32, (1, n_cmp), 1) == n_cmp - 1, 0.0, pltpu.roll(s16, n_cmp - 1, 1))
    cmp = (s16 + nxt) / float(CMP_BLOCK)
    lane_c = lax.broadcasted_iota(I32, (G_D, n_cmp), 1)
    cmask = (lane_c * CMP_STRIDE + (CMP_BLOCK - 1)) <= t_now
    n_lane = ovl_ref.shape[1]
    lane = lax.broadcasted_iota(I32, (1, n_lane), 1)
    cur = t_now // SEL_BLOCK
    k_sel = min(TOPK, n_sel)
    q = q_ref[0]
    for kvh in range(KVH_D):
        qs = _stack_heads(q, kvh).astype(BF16)
        kc = _norm_feature_major(cmp[kvh * HD_D:(kvh + 1) * HD_D], kgc_ref[0])
        pc = _softmax2_rows(_dot(qs, kc.astype(BF16)), cmask).astype(BF16)
        o_cmp = _dot_nt(pc, cmp[LANES + kvh * HD_D:LANES + (kvh + 1) * HD_D].astype(BF16))
        for g in range(G_D):
            hd = kvh * G_D + g
            ocmp_ref[0, :, hd * HD_D:(hd + 1) * HD_D] = o_cmp[g:g + 1, :]
        imp = jnp.sum(_dot(pc, ovl_ref[...]), axis=0, keepdims=True)
        forced = (lane == 0) | (lane == cur) | (lane == cur - 1)
        score = jnp.where(lane <= cur, imp + jnp.where(forced, FORCE_BONUS, 0.0), -jnp.inf)
        cnt = _rank_count(score, lane, n_sel, 1)
        lane_f = lane.astype(F32)
        row = jnp.zeros((1, LANES), F32)
        out_lane = lax.broadcasted_iota(I32, (1, LANES), 1)
        for r in range(k_sel):
            pick = jnp.sum(jnp.where(cnt == float(r), lane_f, 0.0), axis=1, keepdims=True)
            row = jnp.where(out_lane == r, pick, row)
        idx_ref[0, kvh:kvh + 1, :] = row.astype(I32)


def _nsa_dec_select(nq, s16_all, layer, P, C, t_now):
    Bd = nq.shape[0]
    n16 = s16_all.shape[3]
    n_sel = C['n_sel_d']
    n_lane = C['ovl_d'].shape[1]
    return pl.pallas_call(
        functools.partial(_nsa_dec_select_kernel, t_now, n_sel),
        grid=(Bd,),
        in_specs=[pl.BlockSpec((1, 1, H_D * HD_D), lambda b: (b, 0, 0)),
                  pl.BlockSpec((1, 1, KV_ROW, n16), lambda b: (layer, b, 0, 0)),
                  pl.BlockSpec((None, 3, HD_D, 1), lambda b: (layer, 0, 0, 0)),
                  pl.BlockSpec((n16, n_lane), lambda b: (0, 0))],
        out_specs=[pl.BlockSpec((1, 1, H_D * HD_D), lambda b: (b, 0, 0)),
                   pl.BlockSpec((1, KVH_D, LANES), lambda b: (b, 0, 0))],
        out_shape=[jax.ShapeDtypeStruct((Bd, 1, H_D * HD_D), F32),
                   jax.ShapeDtypeStruct((Bd, KVH_D, LANES), I32)],
        compiler_params=pltpu.CompilerParams(dimension_semantics=("parallel",), vmem_limit_bytes=VMEM_LIMIT),
        name="nsa_dec_select",
    )(nq.reshape(Bd, 1, H_D * HD_D), s16_all, P['kgc'], C['ovl_d'])


def _attend_rows(qs, kt, vt, valid, k_new, v_new):
    s = _dot(qs.astype(BF16), kt.astype(BF16))
    if valid is not None:
        s = jnp.where(valid > 0, s, NEG)
    s_new = jnp.sum(qs * k_new, axis=1, keepdims=True)
    m = jnp.maximum(jnp.max(s, axis=-1, keepdims=True), s_new)
    p = jnp.exp2(s - m)
    if valid is not None:
        p = jnp.where(valid > 0, p, 0.0)
    p_new = jnp.exp2(s_new - m)
    den = jnp.sum(p, axis=-1, keepdims=True) + p_new
    return (_dot_nt(p.astype(BF16), vt.astype(BF16)) + p_new * v_new) / den


def _nsa_dec_attend_kernel(n_past_blocks, k_sel, pt_ref, ix_ref, q_ref, tail_ref, ocmp_ref, slcn_ref, winn_ref,
                           winc_ref, *refs):
    blk_refs = refs[:2 * KVH_D * k_sel]
    o_ref = refs[2 * KVH_D * k_sel]
    b = pl.program_id(0)
    q = q_ref[0]
    gates = tail_ref[0]
    ocmp = ocmp_ref[0]
    slc_new = slcn_ref[0]
    win_new = winn_ref[0]
    per_page = PAGE_SIZE // SEL_BLOCK
    lane_pos = lax.broadcasted_iota(I32, (G_D, k_sel * PAGE_SIZE), 1)
    lane_slot = lane_pos >> PAGE_SHIFT
    lane_sub = (lane_pos >> SEL_SHIFT) & (per_page - 1)
    for kvh in range(KVH_D):
        ksl = slice(kvh * HD_D, (kvh + 1) * HD_D)
        vsl = slice(LANES + kvh * HD_D, LANES + (kvh + 1) * HD_D)
        qs = _stack_heads(q, kvh)
        slab = lambda kv: jnp.concatenate(
            [blk_refs[(kvh * k_sel + r) * 2 + kv][0] for r in range(k_sel)], axis=1)
        kt, vt = slab(0), slab(1)
        valid = jnp.zeros((G_D, k_sel * PAGE_SIZE), I32)
        for r in range(k_sel):
            blk = ix_ref[b, kvh * LANES + r]
            in_past = (blk < n_past_blocks).astype(I32)
            valid = jnp.where((lane_slot == r) & (lane_sub == blk % per_page), in_past, valid)
        o_slc = _attend_rows(qs, kt, vt, valid, slc_new[:, ksl], slc_new[:, vsl])
        o_win = _attend_rows(qs, winc_ref[0, ksl, :], winc_ref[0, vsl, :], None, win_new[:, ksl], win_new[:, vsl])
        for g in range(G_D):
            hd = kvh * G_D + g
            gc = TAIL_GATE0 + hd * 3
            hsl = slice(hd * HD_D, (hd + 1) * HD_D)
            o_ref[0, :, hsl] = (gates[:, gc:gc + 1] * ocmp[:, hsl] + gates[:, gc + 1:gc + 2] * o_slc[g:g + 1, :]
                                + gates[:, gc + 2:gc + 3] * o_win[g:g + 1, :])


def _nsa_dec_attend(nq, tail, ocmp, idx, slc_new, win_new, cache_slc, cache_win, page_table, layer, C):
    Bd = nq.shape[0]
    depth, n_phys = cache_slc.shape[0], cache_slc.shape[1]
    n_past_blocks = page_table.shape[1] * (PAGE_SIZE // SEL_BLOCK)
    k_sel = min(TOPK, C['n_sel_d'])
    per_page = PAGE_SIZE // SEL_BLOCK
    slc_flat = cache_slc.reshape(depth * n_phys * 2 * KVH_D, HD_D, PAGE_SIZE)
    n_win = cache_win.shape[-1]
    win_flat = cache_win.reshape(depth * Bd, KV_ROW, n_win)

    def blk_spec(kvh, r, kv):
        def imap(b, pt, ix):
            blk = jnp.minimum(ix[b, kvh * LANES + r], n_past_blocks - 1)
            return (((layer * n_phys + pt[b, blk // per_page]) * 2 + kv) * KVH_D + kvh, 0, 0)
        return pl.BlockSpec((1, HD_D, PAGE_SIZE), imap)

    one = lambda w: pl.BlockSpec((1, 1, w), lambda b, pt, ix: (b, 0, 0))
    r3 = lambda a: a.reshape(Bd, 1, a.shape[-1])
    o = pl.pallas_call(
        functools.partial(_nsa_dec_attend_kernel, n_past_blocks, k_sel),
        grid_spec=pltpu.PrefetchScalarGridSpec(
            num_scalar_prefetch=2,
            grid=(Bd,),
            in_specs=[one(H_D * HD_D), one(LANES), one(H_D * HD_D), one(KV_ROW), one(KV_ROW),
                      pl.BlockSpec((1, KV_ROW, n_win), lambda b, pt, ix: (layer * Bd + b, 0, 0))]
                     + [blk_spec(kvh, r, kv) for kvh in range(KVH_D) for r in range(k_sel) for kv in range(2)],
            out_specs=one(H_D * HD_D)),
        out_shape=jax.ShapeDtypeStruct((Bd, 1, H_D * HD_D), F32),
        compiler_params=pltpu.CompilerParams(dimension_semantics=("arbitrary",), vmem_limit_bytes=VMEM_LIMIT),
        name="nsa_dec_attend",
    )(page_table, idx.reshape(Bd, KVH_D * LANES), r3(nq), r3(tail), ocmp, r3(slc_new), r3(win_new), win_flat,
      *([slc_flat] * (2 * KVH_D * k_sel)))
    return o.reshape(Bd, H_D * HD_D)


def _merge_kernel(x_ref, b0_ref, b1_ref, b2_ref, b3_ref, n1_ref, wg_ref, bg_ref, wb_ref, wo_ref, o_ref):
    x = x_ref[...]
    D = x.shape[1]
    hb = _rms(x, n1_ref[...]).astype(BF16)
    acc = jnp.zeros(x.shape, F32)
    for n, br in enumerate((b0_ref, b1_ref, b2_ref, b3_ref)):
        gate = _sigmoid(_dot(hb, wg_ref[:, n * D:(n + 1) * D]) + bg_ref[:, n * D:(n + 1) * D])
        acc = acc + gate * _dot(br[...].astype(BF16), wb_ref[n])
    o_ref[...] = x + _dot(acc.astype(BF16), wo_ref[...])


def _merge(x2d, branches, layer, P, tm):
    T, D = x2d.shape
    row = lambda w: pl.BlockSpec((tm, w), lambda i: (i, 0))
    lay2 = lambda a, b: pl.BlockSpec((None, a, b), lambda i: (layer, 0, 0))
    return pl.pallas_call(
        _merge_kernel,
        grid=(T // tm,),
        in_specs=[row(D)] + [row(b.shape[1]) for b in branches]
                 + [lay2(1, D), lay2(D, N_BRANCH * D), lay2(1, N_BRANCH * D),
                    pl.BlockSpec((None, N_BRANCH, D // 2, D), lambda i: (layer, 0, 0, 0)), lay2(D, D)],
        out_specs=row(D),
        out_shape=jax.ShapeDtypeStruct((T, D), F32),
        compiler_params=pltpu.CompilerParams(dimension_semantics=("parallel",), vmem_limit_bytes=VMEM_LIMIT),
        name="merge",
    )(x2d, *branches, P['n1'], P['w_gate'], P['b_gate'], P['w_branch'], P['w_out'])


def _mlp_kernel(x_ref, n2_ref, w1_ref, w2_ref, o_ref):
    x = x_ref[...]
    u = jnp.maximum(_dot(_rms(x, n2_ref[...]).astype(BF16), w1_ref[...]), 0.0)
    o_ref[...] = x + _dot((u * u).astype(BF16), w2_ref[...])


def _mlp(x2d, layer, P, tm):
    T, D = x2d.shape
    dff = P['w1'].shape[2]
    row = pl.BlockSpec((tm, D), lambda i: (i, 0))
    return pl.pallas_call(
        _mlp_kernel,
        grid=(T // tm,),
        in_specs=[row, pl.BlockSpec((None, 1, D), lambda i: (layer, 0, 0)),
                  pl.BlockSpec((None, D, dff), lambda i: (layer, 0, 0)),
                  pl.BlockSpec((None, dff, D), lambda i: (layer, 0, 0))],
        out_specs=row,
        out_shape=jax.ShapeDtypeStruct((T, D), F32),
        compiler_params=pltpu.CompilerParams(dimension_semantics=("parallel",), vmem_limit_bytes=VMEM_LIMIT),
        name="mlp",
    )(x2d, P['n2'], P['w1'], P['w2'])


def _prep_params(norm1_g, w_in, gla_wa2, gla_ba, gla_norm_g, hgrn_lb_param, hgrn_norm_g, pool_w, pool_scale,
                 nsa_q_norm_g, nsa_k_norm_g, w_branch, w_gate, b_gate, w_out, norm2_g, mlp_w1, mlp_w2):
    depth = w_in.shape[0]
    sizes = (QK, QK, VW, VW, GLA_RANK, QK, QK, VW, VW, 512, H_D * HD_D) + (KVH_D * HD_D,) * 6 + (H_D * 3,)
    offs = [0]
    for s in sizes:
        offs.append(offs[-1] + s)
    col = lambda i: w_in[:, :, offs[i]:offs[i + 1]]
    order = [0, 1, 2, 3, 5, 6, 7, 8, 9, 10, 11, 12, 13, 14, 15, 16, 4, 17]
    pad = jnp.zeros(w_in.shape[:2] + (LANES - GLA_RANK - H_D * 3,), w_in.dtype)
    w_in_r = jnp.concatenate([col(i) for i in order] + [pad], axis=-1).astype(BF16)
    wa2 = jnp.concatenate([gla_wa2, jnp.zeros((depth, LANES - GLA_RANK, QK), gla_wa2.dtype)], axis=1).astype(BF16)
    r3 = lambda a: a.reshape(depth, 1, a.shape[-1])
    lane = jnp.arange(LANES)
    seg = jnp.where((lane[:, None] // HD_D) == (lane[None, :] // HD_D), 1.0 / HD_D, 0.0).astype(BF16)
    w_kvt = jnp.transpose(w_in_r[:, :, OFF_KV:OFF_TAIL], (0, 2, 1))
    return dict(
        n1=r3(norm1_g), w_in=w_in_r, w_kvt=w_kvt, wa2=wa2, ba=r3(gla_ba), lbp=hgrn_lb_param,
        qg=r3(jnp.tile(nsa_q_norm_g, (1, H_D))), kg=jnp.tile(nsa_k_norm_g, (1, 1, KVH_D)), seg=seg,
        kgc=nsa_k_norm_g.reshape(depth, 3, HD_D, 1),
        gla_g=gla_norm_g, hgrn_g=hgrn_norm_g, pool_w=pool_w.astype(BF16), pool_sc=r3(pool_scale),
        w_gate=w_gate.astype(BF16), b_gate=r3(b_gate), w_branch=w_branch.astype(BF16), w_out=w_out.astype(BF16),
        n2=r3(norm2_g), w1=mlp_w1.astype(BF16), w2=mlp_w2.astype(BF16))


def _overlap(n_cmp_rows, n_lanes):
    per = SEL_BLOCK // CMP_STRIDE
    ci = jnp.arange(n_cmp_rows)[:, None]
    sj = jnp.arange(n_lanes)[None, :]
    return jnp.where((ci // per == sj) | ((ci + 1) // per == sj), 1.0, 0.0).astype(BF16)


def _prep_consts(L, past):
    TQ = 128
    KC = 512
    kpos = jnp.arange(L).reshape(L // KC, 1, KC)
    exp_p = jnp.where(kpos // SEL_BLOCK == jnp.arange(LANES)[None, :, None], 1.0, 0.0).astype(BF16)
    n16_d = past // CMP_STRIDE
    n_sel_d = (past + 1 + SEL_BLOCK - 1) // SEL_BLOCK
    n_lane_d = -(-n_sel_d // LANES) * LANES
    r = jnp.arange(L)[:, None]
    c0 = jnp.arange(L // CMP_STRIDE)[None, :] * CMP_STRIDE
    pool32_p = jnp.where((r >= c0) & (r < c0 + CMP_BLOCK), 1.0, 0.0).astype(BF16)
    return dict(TQ=TQ, KC=KC, ovl_p=_overlap(L // CMP_STRIDE, L // SEL_BLOCK).T, exp_p=exp_p, pool32_p=pool32_p,
                ovl_d=_overlap(n16_d, n_lane_d), n_sel_d=n_sel_d)


def kernel(x_prompt, x_sample, state_gla, state_hgrn, state_pool, cache_cmp_kv, cache_slc_kv, cache_win_kv,
           page_table, norm1_g, w_in, gla_wa2, gla_ba, gla_norm_g, hgrn_lb_param, hgrn_norm_g, pool_w,
           pool_scale, nsa_q_norm_g, nsa_k_norm_g, w_branch, w_gate, b_gate, w_out, norm2_g, mlp_w1, mlp_w2):
    B, L, D = x_prompt.shape
    Bd = x_sample.shape[0]
    depth = w_in.shape[0]
    n_pages = page_table.shape[1]
    past = n_pages * PAGE_SIZE
    assert x_sample.shape[1] == 1 and L % 512 == 0 and L >= WINDOW + 128 and past % SEL_BLOCK == 0
    assert cache_win_kv.shape[2] == WINDOW and past >= WINDOW

    P = _prep_params(norm1_g, w_in, gla_wa2, gla_ba, gla_norm_g, hgrn_lb_param, hgrn_norm_g, pool_w, pool_scale,
                     nsa_q_norm_g, nsa_k_norm_g, w_branch, w_gate, b_gate, w_out, norm2_g, mlp_w1, mlp_w2)
    C = _prep_consts(L, past)
    tm_p = 256
    cb = 128
    bb = 8 if Bd % 8 == 0 else Bd
    keep = min(WINDOW, L)

    pps = 16 if n_pages % 16 == 0 else n_pages
    odd = page_table[0, 0] | 1
    one = (odd // odd).astype(F32)
    cmp_fm = _feature_major(cache_cmp_kv) * one
    slc_fm = _feature_major(cache_slc_kv) * one
    win_fm = _feature_major(cache_win_kv) * one
    s16_all = _page_sums(cmp_fm, page_table, pps)

    xp = x_prompt.reshape(B * L, D)
    xs = x_sample.reshape(Bd, D)
    outs = {k: [] for k in ('gla_p', 'gla_s', 'hgrn_p', 'hgrn_s', 'pool_p', 'pool_s', 'cmp_p', 'cmp_s',
                            'slc_p', 'slc_s', 'win_p', 'win_s')}
    zero_prev = jnp.zeros((B, POOL_HIST + 1, 512), F32)
    rows5 = lambda a, n: a.reshape(n, -1, 2, KVH_D, HD_D)
    for l in range(depth):
        gqkg, gvr, hqkg, hvr, u, nq, cmp_t, slc_c, win_c, tail = _in_proj(xp, l, P, tm_p, seq_len=L, kc=C['KC'])
        b3 = lambda a: a.reshape(B, L, a.shape[-1])
        o_gla, s_gla = _gla_prompt(b3(gqkg), b3(gvr), P['gla_g'][l:l + 1], cb)
        o_hgrn, s_hgrn = _gla_prompt(b3(hqkg), b3(hvr), P['hgrn_g'][l:l + 1], cb)
        o_pool = _pool(b3(u), zero_prev, l, P, 0)
        kcvc = _cmp_prompt(cmp_t, l, P, C)
        o_nsa = _nsa_prompt(b3(nq), b3(tail), kcvc, slc_c, win_c, C)
        flat = lambda a: a.reshape(B * L, a.shape[-1])
        xp = _merge(xp, (flat(o_gla), flat(o_hgrn), flat(o_pool), flat(o_nsa)), l, P, tm_p)
        xp = _mlp(xp, l, P, tm_p)
        outs['gla_p'].append(s_gla)
        outs['hgrn_p'].append(s_hgrn)
        outs['pool_p'].append(b3(u)[:, L - POOL_HIST:])
        unchunk = lambda a: jnp.transpose(a, (0, 2, 1, 3)).reshape(B, KV_ROW, -1)
        outs['cmp_p'].append(cmp_t)
        outs['slc_p'].append(unchunk(slc_c))
        outs['win_p'].append(unchunk(win_c[:, (L - keep) // LANES:]))

        gqkg, gvr, hqkg, hvr, u, nq, cmp_r, slc_r, win_r, tail = _in_proj(xs, l, P, Bd)
        o_gla, s_gla = _gla_decode(gqkg, gvr, P['gla_g'][l], state_gla, l, bb)
        o_hgrn, s_hgrn = _gla_decode(hqkg, hvr, P['hgrn_g'][l], state_hgrn, l, bb)
        prev = state_pool[l]
        prev16 = jnp.concatenate([jnp.zeros((Bd, 1, 512), F32), prev], axis=1)
        o_pool = _pool(u.reshape(Bd, 1, 512), prev16, l, P, past).reshape(Bd, 512)
        ocmp, idx = _nsa_dec_select(nq, s16_all, l, P, C, past)
        o_nsa = _nsa_dec_attend(nq, tail, ocmp, idx, slc_r, win_r, slc_fm, win_fm, page_table, l, C)
        xs = _merge(xs, (o_gla, o_hgrn, o_pool, o_nsa), l, P, Bd)
        xs = _mlp(xs, l, P, Bd)
        outs['gla_s'].append(s_gla)
        outs['hgrn_s'].append(s_hgrn)
        outs['pool_s'].append(jnp.concatenate([prev[:, 1:], u.reshape(Bd, 1, 512)], axis=1))
        outs['cmp_s'].append(rows5(cmp_r, Bd))
        outs['slc_s'].append(rows5(slc_r, Bd))
        outs['win_s'].append(rows5(win_r, Bd))

    st = lambda k: jnp.stack(outs[k])

    def rows_p(k):
        a = st(k)
        return jnp.transpose(a.reshape(depth, B, 2, KVH_D, HD_D, a.shape[-1]), (0, 1, 5, 2, 3, 4))

    return (xp.reshape(B, L, D), xs.reshape(Bd, 1, D),
            st('gla_p'), st('gla_s'), st('hgrn_p'), st('hgrn_s'), st('pool_p'), st('pool_s'),
            rows_p('cmp_p'), st('cmp_s'), rows_p('slc_p'), st('slc_s'), rows_p('win_p'), st('win_s'))
```

```python
import functools

import jax
import jax.numpy as jnp
from jax import lax
from jax.experimental import pallas as pl
from jax.experimental.pallas import tpu as pltpu

F32 = jnp.float32
BF16 = jnp.bfloat16
I32 = jnp.int32

N_BRANCH = 4
H_LIN = 4
DK_LIN = 64
DV_LIN = 128
GLA_RANK = 16
GLA_TAU = 16.0
POOL_WINDOWS = (2, 4, 8, 16)
POOL_HIST = 15
POOL_GROUP = 128
H_D = 8
HD_D = 64
KVH_D = 2
G_D = 4
CMP_STRIDE = 16
CMP_BLOCK = 32
SEL_BLOCK = 64
SEL_SHIFT = 6
TOPK = 16
WINDOW = 512
FORCE_BONUS = 1000.0
PAGE_SIZE = 128
PAGE_SHIFT = 7
EPS = 1e-6
NEG = -1e30
LOG2E = 1.4426950408889634

LANES = 128
VMEM_LIMIT = 56 * 1024 * 1024

QK = H_LIN * DK_LIN
VW = H_LIN * DV_LIN
OFF_GLA = 0
OFF_HGRN = OFF_GLA + 2 * QK + 2 * VW
OFF_POOL = OFF_HGRN + 2 * QK + 2 * VW
OFF_NQ = OFF_POOL + 512
OFF_KV = OFF_NQ + H_D * HD_D
OFF_TAIL = OFF_KV + 6 * KVH_D * HD_D
IN_PAD = OFF_TAIL + LANES
TAIL_GATE0 = GLA_RANK
KV_ROW = 2 * KVH_D * HD_D


def _dot(a, b):
    return jnp.dot(a, b, preferred_element_type=F32)


def _dot_nt(a, b):
    return lax.dot_general(a, b, (((1,), (1,)), ((), ())), preferred_element_type=F32)


def _rms(x, gain):
    return x * lax.rsqrt(jnp.mean(x * x, axis=-1, keepdims=True) + EPS) * gain


def _seg_mean(x2, seg):
    parts = []
    for c in range(x2.shape[1] // LANES):
        xc = x2[:, c * LANES:(c + 1) * LANES]
        h1 = xc.astype(BF16)
        r1 = xc - h1.astype(F32)
        h2 = r1.astype(BF16)
        h3 = (r1 - h2.astype(F32)).astype(BF16)
        parts.append(_dot(h1, seg) + _dot(h2, seg) + _dot(h3, seg))
    return parts[0] if len(parts) == 1 else jnp.concatenate(parts, axis=1)


def _seg_rms(x, gain, seg):
    return x * lax.rsqrt(_seg_mean(x * x, seg) + EPS) * gain


def _sigmoid(x):
    return 1.0 / (1.0 + jnp.exp(-x))


def _in_proj_kernel(layer, kv_t, x_ref, n1_ref, w_ref, wa2_ref, ba_ref, lbp_ref, qg_ref, kg_ref, seg_ref,
                    wkvt_ref, kgc_ref,
                    gqkg_ref, gvr_ref, hqkg_ref, hvr_ref, u_ref, nq_ref, cmp_ref, slc_ref, win_ref, tail_ref):
    x = x_ref[...]
    hb = _rms(x, n1_ref[...]).astype(BF16)
    seg = seg_ref[...]

    def proj(c0, c1):
        return _dot(hb, w_ref[:, c0:c1])

    tail = proj(OFF_TAIL, IN_PAD)
    tail_ref[...] = _sigmoid(tail)

    a_pre = _dot(tail.astype(BF16), wa2_ref[...]) + ba_ref[...]
    log_a = (jnp.minimum(a_pre, 0.0) - jnp.log(1.0 + jnp.exp(-jnp.abs(a_pre)))) / GLA_TAU
    z = proj(OFF_GLA, OFF_GLA + 2 * QK)
    gqkg_ref[:, 0:QK] = z[:, 0:QK] * (DK_LIN ** -0.5)
    gqkg_ref[:, QK:2 * QK] = z[:, QK:2 * QK]
    gqkg_ref[:, 2 * QK:3 * QK] = log_a
    gvr_ref[...] = proj(OFF_GLA + 2 * QK, OFF_HGRN)

    lbp = lbp_ref[...]
    pe = jnp.exp(lbp - jnp.max(lbp, axis=0, keepdims=True))
    soft = pe / jnp.sum(pe, axis=0, keepdims=True)
    lb = jnp.zeros((1, QK), F32)
    for i in range(1, layer + 1):
        lb = lb + soft[i:i + 1, :]
    z = proj(OFF_HGRN, OFF_HGRN + 2 * QK)
    hq = z[:, 0:QK]
    fgate = lb + (1.0 - lb) * _sigmoid(z[:, QK:2 * QK])
    hqkg_ref[:, 0:QK] = hq * _sigmoid(hq) * (DK_LIN ** -0.5)
    hqkg_ref[:, QK:2 * QK] = 1.0 - fgate
    hqkg_ref[:, 2 * QK:3 * QK] = jnp.log(fgate)
    hvr_ref[...] = proj(OFF_HGRN + 2 * QK, OFF_POOL)

    u_ref[...] = proj(OFF_POOL, OFF_NQ)
    nq_ref[...] = _seg_rms(proj(OFF_NQ, OFF_KV), qg_ref[...], seg)

    if kv_t:
        zt = _dot_nt(wkvt_ref[...], hb)
        cmp_ref[0] = zt[0:KV_ROW]

        def normed_rows(base, gi):
            ks = [_norm_feature_major(zt[base + kvh * HD_D:base + (kvh + 1) * HD_D], kgc_ref[gi])
                  for kvh in range(KVH_D)]
            return jnp.concatenate(ks + [zt[base + LANES:base + KV_ROW]], axis=0)

        slc_ref[0, 0] = normed_rows(KV_ROW, 1)
        win = normed_rows(2 * KV_ROW, 2)
        for c in range(win_ref.shape[1]):
            win_ref[0, c] = win[:, c * LANES:(c + 1) * LANES]
    else:
        zkv = proj(OFF_KV, OFF_TAIL)
        cmp_ref[...] = zkv[:, 0:KV_ROW]
        slc_ref[:, 0:LANES] = _seg_rms(zkv[:, 256:384], kg_ref[1:2, :], seg)
        slc_ref[:, LANES:KV_ROW] = zkv[:, 384:512]
        win_ref[:, 0:LANES] = _seg_rms(zkv[:, 512:640], kg_ref[2:3, :], seg)
        win_ref[:, LANES:KV_ROW] = zkv[:, 640:768]


def _in_proj(x2d, layer, P, tm, seq_len=None, kc=None):
    T, D = x2d.shape
    depth = P['lbp'].shape[0]
    kv_t = seq_len is not None
    row = lambda w: pl.BlockSpec((tm, w), lambda i: (i, 0))
    lay2 = lambda a, b: pl.BlockSpec((None, a, b), lambda i: (layer, 0, 0))
    widths = (3 * QK, 2 * VW, 3 * QK, 2 * VW, 512, H_D * HD_D)
    out_specs = [row(w) for w in widths]
    out_shape = [jax.ShapeDtypeStruct((T, w), F32) for w in widths]
    if kv_t:
        nt = seq_len // tm
        per = kc // tm
        nb = T // seq_len
        out_specs += [pl.BlockSpec((1, KV_ROW, tm), lambda i: (i // nt, 0, i % nt)),
                      pl.BlockSpec((1, 1, KV_ROW, tm), lambda i: (i // nt, (i % nt) // per, 0, (i % nt) % per)),
                      pl.BlockSpec((1, tm // LANES, KV_ROW, LANES), lambda i: (i // nt, i % nt, 0, 0))]
        out_shape += [jax.ShapeDtypeStruct((nb, KV_ROW, seq_len), F32),
                      jax.ShapeDtypeStruct((nb, seq_len // kc, KV_ROW, kc), F32),
                      jax.ShapeDtypeStruct((nb, seq_len // LANES, KV_ROW, LANES), F32)]
    else:
        out_specs += [row(KV_ROW)] * 3
        out_shape += [jax.ShapeDtypeStruct((T, KV_ROW), F32)] * 3
    out_specs.append(row(LANES))
    out_shape.append(jax.ShapeDtypeStruct((T, LANES), F32))
    return pl.pallas_call(
        functools.partial(_in_proj_kernel, layer, kv_t),
        grid=(T // tm,),
        in_specs=[row(D), lay2(1, D), lay2(D, IN_PAD), lay2(LANES, QK), lay2(1, QK),
                  pl.BlockSpec((depth, QK), lambda i: (0, 0)), lay2(1, H_D * HD_D), lay2(3, LANES),
                  pl.BlockSpec((LANES, LANES), lambda i: (0, 0)), lay2(3 * KV_ROW, D),
                  pl.BlockSpec((None, 3, HD_D, 1), lambda i: (layer, 0, 0, 0))],
        out_specs=out_specs,
        out_shape=out_shape,
        compiler_params=pltpu.CompilerParams(dimension_semantics=("parallel",), vmem_limit_bytes=VMEM_LIMIT),
        name="in_proj",
    )(x2d, P['n1'], P['w_in'], P['wa2'], P['ba'], P['lbp'], P['qg'], P['kg'], P['seg'], P['w_kvt'], P['kgc'])


def _gla_kernel(levels, qkg_ref, vr_ref, gain_ref, o_ref, s_ref):
    CB = qkg_ref.shape[1]

    @pl.when(pl.program_id(1) == 0)
    def _():
        s_ref[...] = jnp.zeros_like(s_ref)

    q = qkg_ref[0, :, 0:QK]
    k = qkg_ref[0, :, QK:2 * QK]
    g = qkg_ref[0, :, 2 * QK:3 * QK]
    row = lax.broadcasted_iota(I32, (CB, 1), 0)
    r2 = lax.broadcasted_iota(I32, (CB, CB), 0)
    c2 = lax.broadcasted_iota(I32, (CB, CB), 1)
    heads = [slice(h * DK_LIN, (h + 1) * DK_LIN) for h in range(H_LIN)]

    qb = q.astype(BF16)
    kb = k.astype(BF16)
    att = [jnp.where(r2 == c2, _dot_nt(qb[:, hs], kb[:, hs]), 0.0) for hs in heads]
    c = g
    tot = g
    for lvl in range(levels):
        half = 1 << lvl
        upper = ((row >> lvl) & 1) == 1
        qt = jnp.where(upper, q * jnp.exp(c), 0.0).astype(BF16)
        kt = jnp.where(upper, 0.0, k * jnp.exp(tot - c)).astype(BF16)
        same = (r2 >> (lvl + 1)) == (c2 >> (lvl + 1))
        for h, hs in enumerate(heads):
            att[h] = att[h] + jnp.where(same, _dot_nt(qt[:, hs], kt[:, hs]), 0.0)
        t_lo = pltpu.roll(tot, half, 0)
        t_hi = pltpu.roll(tot, CB - half, 0)
        c = c + jnp.where(upper, t_lo, 0.0)
        tot = tot + jnp.where(upper, t_lo, t_hi)

    qd = (q * jnp.exp(c)).astype(BF16)
    kdt = jnp.transpose(k * jnp.exp(tot - c)).astype(BF16)
    eye = (lax.broadcasted_iota(I32, (DK_LIN, DK_LIN), 0) == lax.broadcasted_iota(I32, (DK_LIN, DK_LIN), 1))
    gain = gain_ref[...]
    for h, hs in enumerate(heads):
        vs = slice(h * DV_LIN, (h + 1) * DV_LIN)
        v = vr_ref[0, :, vs].astype(BF16)
        s_old = s_ref[0, h]
        o = _dot(att[h].astype(BF16), v) + _dot(qd[:, hs], s_old.astype(BF16))
        t_row = jnp.broadcast_to(tot[0:1, hs], (DK_LIN, DK_LIN))
        t_col = jnp.sum(jnp.where(eye, t_row, 0.0), axis=1, keepdims=True)
        s_ref[0, h] = s_old * jnp.exp(t_col) + _dot(kdt[hs, :], v)
        r = vr_ref[0, :, VW + h * DV_LIN:VW + (h + 1) * DV_LIN]
        o_ref[0, :, vs] = _rms(o, gain) * (r * _sigmoid(r))


def _gla_prompt(qkg, vr, gain, cb):
    B, L, _ = qkg.shape
    levels = cb.bit_length() - 1
    return pl.pallas_call(
        functools.partial(_gla_kernel, levels),
        grid=(B, L // cb),
        in_specs=[pl.BlockSpec((1, cb, 3 * QK), lambda b, j: (b, j, 0)),
                  pl.BlockSpec((1, cb, 2 * VW), lambda b, j: (b, j, 0)),
                  pl.BlockSpec((1, DV_LIN), lambda b, j: (0, 0))],
        out_specs=[pl.BlockSpec((1, cb, VW), lambda b, j: (b, j, 0)),
                   pl.BlockSpec((1, H_LIN, DK_LIN, DV_LIN), lambda b, j: (b, 0, 0, 0))],
        out_shape=[jax.ShapeDtypeStruct((B, L, VW), F32),
                   jax.ShapeDtypeStruct((B, H_LIN, DK_LIN, DV_LIN), F32)],
        compiler_params=pltpu.CompilerParams(dimension_semantics=("parallel", "arbitrary"),
                                             vmem_limit_bytes=VMEM_LIMIT),
        name="gla_prompt",
    )(qkg, vr, gain)


def _gla_dec_kernel(q_ref, k_ref, g_ref, v_ref, r_ref, gain_ref, s0_ref, o_ref, s_ref):
    s_new = s0_ref[...] * jnp.exp(g_ref[...]) + k_ref[...] * v_ref[...]
    s_ref[...] = s_new
    o = jnp.sum(q_ref[...] * s_new, axis=2, keepdims=True)
    r = r_ref[...]
    o_ref[...] = _rms(o, gain_ref[...]) * (r * _sigmoid(r))


def _gla_decode(qkg, vr, gain, state, layer, bb):
    Bd = qkg.shape[0]
    col = lambda a: a.reshape(Bd, H_LIN, DK_LIN, 1)
    rowv = lambda a: a.reshape(Bd, H_LIN, 1, DV_LIN)
    q, k, g = col(qkg[:, 0:QK]), col(qkg[:, QK:2 * QK]), col(qkg[:, 2 * QK:])
    v, r = rowv(vr[:, 0:VW]), rowv(vr[:, VW:])
    cspec = pl.BlockSpec((bb, H_LIN, DK_LIN, 1), lambda i: (i, 0, 0, 0))
    rspec = pl.BlockSpec((bb, H_LIN, 1, DV_LIN), lambda i: (i, 0, 0, 0))
    o, s_new = pl.pallas_call(
        _gla_dec_kernel,
        grid=(Bd // bb,),
        in_specs=[cspec, cspec, cspec, rspec, rspec,
                  pl.BlockSpec((1, 1, 1, DV_LIN), lambda i: (0, 0, 0, 0)),
                  pl.BlockSpec((None, bb, H_LIN, DK_LIN, DV_LIN), lambda i: (layer, i, 0, 0, 0))],
        out_specs=[rspec, pl.BlockSpec((bb, H_LIN, DK_LIN, DV_LIN), lambda i: (i, 0, 0, 0))],
        out_shape=[jax.ShapeDtypeStruct((Bd, H_LIN, 1, DV_LIN), F32),
                   jax.ShapeDtypeStruct((Bd, H_LIN, DK_LIN, DV_LIN), F32)],
        compiler_params=pltpu.CompilerParams(dimension_semantics=("parallel",), vmem_limit_bytes=VMEM_LIMIT),
        name="gla_decode",
    )(q, k, g, v, r, gain.reshape(1, 1, 1, DV_LIN), state)
    return o.reshape(Bd, VW), s_new


def _pool_kernel(pos0, rt, u_ref, prev_ref, w_ref, sc_ref, o_ref, ext_ref):
    L = u_ref.shape[1]
    hdr = POOL_HIST + 1
    ext_ref[0:hdr, :] = prev_ref[0]
    ext_ref[hdr:hdr + L, :] = u_ref[0]
    for r0 in range(0, L, rt):
        n_avail = (pos0 + r0 + 1 + lax.broadcasted_iota(I32, (rt, 1), 0)).astype(F32)
        for gi, w in enumerate(POOL_WINDOWS):
            cs = slice(gi * POOL_GROUP, (gi + 1) * POOL_GROUP)
            tok = ext_ref[hdr + r0:hdr + r0 + rt, cs]
            acc = tok
            for j in range(1, w):
                acc = acc + ext_ref[hdr + r0 - j:hdr + r0 - j + rt, cs]
            d = acc / jnp.minimum(float(w), n_avail) - tok
            o_ref[0, r0:r0 + rt, cs] = _dot(d.astype(BF16), w_ref[gi]) * sc_ref[:, cs]


def _pool(u, prev16, layer, P, pos0):
    B, L, C = u.shape
    rt = min(L, 256)
    return pl.pallas_call(
        functools.partial(_pool_kernel, pos0, rt),
        grid=(B,),
        in_specs=[pl.BlockSpec((1, L, C), lambda b: (b, 0, 0)),
                  pl.BlockSpec((1, POOL_HIST + 1, C), lambda b: (b, 0, 0)),
                  pl.BlockSpec((None, len(POOL_WINDOWS), POOL_GROUP, POOL_GROUP), lambda b: (layer, 0, 0, 0)),
                  pl.BlockSpec((None, 1, C), lambda b: (layer, 0, 0))],
        out_specs=pl.BlockSpec((1, L, C), lambda b: (b, 0, 0)),
        out_shape=jax.ShapeDtypeStruct((B, L, C), F32),
        scratch_shapes=[pltpu.VMEM((POOL_HIST + 1 + L, C), F32)],
        compiler_params=pltpu.CompilerParams(dimension_semantics=("parallel",), vmem_limit_bytes=VMEM_LIMIT),
        name="pool",
    )(u, prev16, P['pool_w'], P['pool_sc'])


def _pool_lanes(x, pm):
    hi = x.astype(BF16)
    lo = (x - hi.astype(F32)).astype(BF16)
    return _dot(hi, pm) + _dot(lo, pm)


def _norm_feature_major(x, gain_col):
    return x * lax.rsqrt(jnp.mean(x * x, axis=0, keepdims=True) + EPS) * gain_col


def _cmp_store(cmp, gain_col, o_ref):
    for kvh in range(KVH_D):
        rows = slice(kvh * HD_D, (kvh + 1) * HD_D)
        o_ref[0, rows, :] = _norm_feature_major(cmp[rows], gain_col)
    o_ref[0, LANES:KV_ROW, :] = cmp[LANES:KV_ROW]


def _cmp_prompt_kernel(rows_ref, kgc_ref, pm_ref, o_ref):
    _cmp_store(_pool_lanes(rows_ref[0], pm_ref[...]) / float(CMP_BLOCK), kgc_ref[0], o_ref)


def _cmp_prompt(rows_t, layer, P, C):
    B, _, L = rows_t.shape
    n16 = L // CMP_STRIDE
    return pl.pallas_call(
        _cmp_prompt_kernel,
        grid=(B,),
        in_specs=[pl.BlockSpec((1, KV_ROW, L), lambda b: (b, 0, 0)),
                  pl.BlockSpec((None, 3, HD_D, 1), lambda b: (layer, 0, 0, 0)),
                  pl.BlockSpec((L, n16), lambda b: (0, 0))],
        out_specs=pl.BlockSpec((1, KV_ROW, n16), lambda b: (b, 0, 0)),
        out_shape=jax.ShapeDtypeStruct((B, KV_ROW, n16), F32),
        compiler_params=pltpu.CompilerParams(dimension_semantics=("parallel",), vmem_limit_bytes=VMEM_LIMIT),
        name="cmp_prompt",
    )(rows_t, P['kgc'], C['pool32_p'])


def _feature_major(cache):
    nd = cache.ndim
    return jnp.transpose(cache, tuple(range(nd - 4)) + (nd - 3, nd - 2, nd - 1, nd - 4))


def _page_sum_kernel(n_in, pt_ref, *refs):
    pm_ref, o_ref = refs[n_in], refs[n_in + 1]
    x = jnp.concatenate([refs[i][0] for i in range(n_in)], axis=1)
    o_ref[0, 0] = _dot(x, pm_ref[...])


def _page_sums(cache, page_table, pages_per_step):
    depth, n_phys = cache.shape[0], cache.shape[1]
    Bd, n_pages = page_table.shape
    flat = cache.reshape(depth * n_phys, KV_ROW, PAGE_SIZE)
    per = PAGE_SIZE // CMP_STRIDE
    n_in = pages_per_step
    pos = jnp.arange(n_in * PAGE_SIZE)
    pm = jnp.where(pos[:, None] // CMP_STRIDE == jnp.arange(n_in * per)[None, :], 1.0, 0.0).astype(BF16)

    def page_spec(i):
        return pl.BlockSpec((1, KV_ROW, PAGE_SIZE),
                            lambda l, b, p, pt: (l * n_phys + pt[b, p * n_in + i], 0, 0))

    return pl.pallas_call(
        functools.partial(_page_sum_kernel, n_in),
        grid_spec=pltpu.PrefetchScalarGridSpec(
            num_scalar_prefetch=1,
            grid=(depth, Bd, n_pages // n_in),
            in_specs=[page_spec(i) for i in range(n_in)]
                     + [pl.BlockSpec((n_in * PAGE_SIZE, n_in * per), lambda l, b, p, pt: (0, 0))],
            out_specs=pl.BlockSpec((1, 1, KV_ROW, n_in * per), lambda l, b, p, pt: (l, b, 0, p))),
        out_shape=jax.ShapeDtypeStruct((depth, Bd, KV_ROW, n_pages * per), F32),
        compiler_params=pltpu.CompilerParams(dimension_semantics=("parallel", "parallel", "arbitrary"),
                                             vmem_limit_bytes=VMEM_LIMIT),
        name="page_sums",
    )(page_table, *([flat] * n_in), pm)


def _rank_count(score, idx, n_blocks, axis):
    cnt = jnp.zeros(score.shape, F32)
    for m in range(n_blocks):
        other = score[m:m + 1, :] if axis == 0 else score[:, m:m + 1]
        beats = jnp.where(other > score, 1.0, jnp.where(other == score, jnp.where(idx > m, 1.0, 0.0), 0.0))
        cnt = cnt + beats
    return cnt


def _softmax2_rows(s2, mask):
    sm = jnp.where(mask, s2, NEG)
    p = jnp.where(mask, jnp.exp2(sm - jnp.max(sm, axis=-1, keepdims=True)), 0.0)
    den = jnp.sum(p, axis=-1, keepdims=True)
    return p / jnp.where(den > 0.0, den, 1.0)


def _nsa_prompt_kernel(TQ, KC, n_sel, q_ref, tail_ref, kcvc_ref, slc_ref, win_ref, ovl_ref, exp_ref, o_ref):
    t0 = pl.program_id(1) * TQ
    R = G_D * TQ
    t = t0 + lax.broadcasted_iota(I32, (TQ, 1), 0)
    t4 = t0 + (lax.broadcasted_iota(I32, (R, 1), 0) & (TQ - 1))
    rep = lambda a: jnp.concatenate([a] * G_D, axis=0)
    q = q_ref[0]
    gates = tail_ref[0]
    n_cmp = kcvc_ref.shape[2]
    lane_c = lax.broadcasted_iota(I32, (1, n_cmp), 1)
    cmask4 = (lane_c * CMP_STRIDE + (CMP_BLOCK - 1)) <= t4
    blk = lax.broadcasted_iota(I32, (n_sel, 1), 0)
    cur = (t0 + lax.broadcasted_iota(I32, (1, TQ), 1)) >> SEL_SHIFT
    j_last = t0 // KC
    k_sel = min(TOPK, n_sel)
    add_rows = lambda s, bias: (s.reshape(G_D, TQ, s.shape[1]) + bias[None]).reshape(R, s.shape[1])
    lane_k = lax.broadcasted_iota(I32, (1, KC), 1)
    WK = WINDOW + TQ
    ws = jnp.maximum(t0 - WINDOW, 0)
    wc = ws // LANES
    dist = t - (ws + lax.broadcasted_iota(I32, (1, WK), 1))
    bias_w = jnp.where((dist >= 0) & (dist <= WINDOW), 0.0, NEG)

    for kvh in range(KVH_D):
        ksl = slice(kvh * HD_D, (kvh + 1) * HD_D)
        vsl = slice(LANES + kvh * HD_D, LANES + (kvh + 1) * HD_D)
        qs = (jnp.concatenate([q[:, (kvh * G_D + g) * HD_D:(kvh * G_D + g + 1) * HD_D] for g in range(G_D)],
                              axis=0) * (HD_D ** -0.5 * LOG2E)).astype(BF16)
        pcf = _softmax2_rows(_dot(qs, kcvc_ref[0, ksl, :].astype(BF16)), cmask4)
        o_cmp = _dot_nt(pcf.astype(BF16), kcvc_ref[0, vsl, :].astype(BF16))
        pcs = pcf[0:TQ]
        for g in range(1, G_D):
            pcs = pcs + pcf[g * TQ:(g + 1) * TQ]
        imp = _dot_nt(ovl_ref[...], pcs.astype(BF16))
        forced = (blk == 0) | (blk == cur) | (blk == cur - 1)
        score = jnp.where(blk <= cur, imp + jnp.where(forced, FORCE_BONUS, 0.0), -jnp.inf)
        cnt = _rank_count(score, blk, n_sel, 0)
        selb = jnp.where(cnt < k_sel, 0.0, NEG)
        selb = jnp.transpose(jnp.concatenate([selb, jnp.zeros((LANES - n_sel, TQ), F32)], axis=0))
        q_aug = jnp.concatenate([rep(selb.astype(BF16)), qs], axis=1)

        def chunk(j, carry, q_aug=q_aug, ksl=ksl, vsl=vsl):
            m_i, l_i, acc = carry
            k_aug = jnp.concatenate([exp_ref[j], slc_ref[0, j, ksl, :].astype(BF16)], axis=0)
            causal = jnp.where(j * KC + lane_k <= t, 0.0, NEG)
            s = add_rows(_dot(q_aug, k_aug), causal)
            m_new = jnp.maximum(m_i, jnp.max(s, axis=-1, keepdims=True))
            alpha = jnp.exp2(m_i - m_new)
            p = jnp.exp2(s - m_new)
            l_new = alpha * l_i + jnp.sum(p, axis=-1, keepdims=True)
            acc = alpha * acc + _dot_nt(p.astype(BF16), slc_ref[0, j, vsl, :].astype(BF16))
            return m_new, l_new, acc

        init = (jnp.full((R, 1), NEG, F32), jnp.zeros((R, 1), F32), jnp.zeros((R, HD_D), F32))
        _, l_f, acc_f = lax.fori_loop(0, j_last + 1, chunk, init)
        o_slc = acc_f / l_f

        win_rows = lambda sl: jnp.concatenate(
            [win_ref[0, wc + i, sl, :].astype(BF16) for i in range(WK // LANES)], axis=1)
        sw = add_rows(_dot(qs, win_rows(ksl)), bias_w)
        pw = jnp.exp2(sw - jnp.max(sw, axis=-1, keepdims=True))
        o_win = _dot_nt(pw.astype(BF16), win_rows(vsl)) / jnp.sum(pw, axis=-1, keepdims=True)

        for g in range(G_D):
            hd = kvh * G_D + g
            rows = slice(g * TQ, (g + 1) * TQ)
            gc = TAIL_GATE0 + hd * 3
            o_ref[0, :, hd * HD_D:(hd + 1) * HD_D] = (gates[:, gc:gc + 1] * o_cmp[rows]
                                                     + gates[:, gc + 1:gc + 2] * o_slc[rows]
                                                     + gates[:, gc + 2:gc + 3] * o_win[rows])


def _nsa_prompt(nq, tail, kcvc, slc, win, C):
    B, L, _ = nq.shape
    TQ, KC = C['TQ'], C['KC']
    n_sel = L // SEL_BLOCK
    n16 = L // CMP_STRIDE
    qspec = lambda w: pl.BlockSpec((1, TQ, w), lambda b, i: (b, i, 0))
    full = lambda n: pl.BlockSpec((1, KV_ROW, n), lambda b, i: (b, 0, 0))
    return pl.pallas_call(
        functools.partial(_nsa_prompt_kernel, TQ, KC, n_sel),
        grid=(B, L // TQ),
        in_specs=[qspec(H_D * HD_D), qspec(LANES), full(n16),
                  pl.BlockSpec((1, L // KC, KV_ROW, KC), lambda b, i: (b, 0, 0, 0)),
                  pl.BlockSpec((1, L // LANES, KV_ROW, LANES), lambda b, i: (b, 0, 0, 0)),
                  pl.BlockSpec((n_sel, n16), lambda b, i: (0, 0)),
                  pl.BlockSpec((L // KC, LANES, KC), lambda b, i: (0, 0, 0))],
        out_specs=qspec(H_D * HD_D),
        out_shape=jax.ShapeDtypeStruct((B, L, H_D * HD_D), F32),
        compiler_params=pltpu.CompilerParams(dimension_semantics=("parallel", "arbitrary"),
                                             vmem_limit_bytes=VMEM_LIMIT),
        name="nsa_prompt",
    )(nq, tail, kcvc, slc, win, C['ovl_p'], C['exp_p'])


def _stack_heads(qrow, kvh):
    return jnp.concatenate([qrow[:, (kvh * G_D + g) * HD_D:(kvh * G_D + g + 1) * HD_D] for g in range(G_D)],
                           axis=0) * (HD_D ** -0.5 * LOG2E)


def _nsa_dec_select_kernel(t_now, n_sel, q_ref, s16_ref, kgc_ref, ovl_ref, ocmp_ref, idx_ref):
    s16 = s16_ref[0, 0]
    n_cmp = s16.shape[1]
    nxt = jnp.where(lax.broadcasted_iota(I32, (1, n_cmp), 1) == n_cmp - 1, 0.0, pltpu.roll(s16, n_cmp - 1, 1))
    cmp = (s16 + nxt) / float(CMP_BLOCK)
    lane_c = lax.broadcasted_iota(I32, (G_D, n_cmp), 1)
    cmask = (lane_c * CMP_STRIDE + (CMP_BLOCK - 1)) <= t_now
    n_lane = ovl_ref.shape[1]
    lane = lax.broadcasted_iota(I32, (1, n_lane), 1)
    cur = t_now // SEL_BLOCK
    k_sel = min(TOPK, n_sel)
    q = q_ref[0]
    for kvh in range(KVH_D):
        qs = _stack_heads(q, kvh).astype(BF16)
        kc = _norm_feature_major(cmp[kvh * HD_D:(kvh + 1) * HD_D], kgc_ref[0])
        pc = _softmax2_rows(_dot(qs, kc.astype(BF16)), cmask).astype(BF16)
        o_cmp = _dot_nt(pc, cmp[LANES + kvh * HD_D:LANES + (kvh + 1) * HD_D].astype(BF16))
        for g in range(G_D):
            hd = kvh * G_D + g
            ocmp_ref[0, :, hd * HD_D:(hd + 1) * HD_D] = o_cmp[g:g + 1, :]
        imp = jnp.sum(_dot(pc, ovl_ref[...]), axis=0, keepdims=True)
        forced = (lane == 0) | (lane == cur) | (lane == cur - 1)
        score = jnp.where(lane <= cur, imp + jnp.where(forced, FORCE_BONUS, 0.0), -jnp.inf)
        cnt = _rank_count(score, lane, n_sel, 1)
        lane_f = lane.astype(F32)
        row = jnp.zeros((1, LANES), F32)
        out_lane = lax.broadcasted_iota(I32, (1, LANES), 1)
        for r in range(k_sel):
            pick = jnp.sum(jnp.where(cnt == float(r), lane_f, 0.0), axis=1, keepdims=True)
            row = jnp.where(out_lane == r, pick, row)
        idx_ref[0, kvh:kvh + 1, :] = row.astype(I32)


def _nsa_dec_select(nq, s16_all, layer, P, C, t_now):
    Bd = nq.shape[0]
    n16 = s16_all.shape[3]
    n_sel = C['n_sel_d']
    n_lane = C['ovl_d'].shape[1]
    return pl.pallas_call(
        functools.partial(_nsa_dec_select_kernel, t_now, n_sel),
        grid=(Bd,),
        in_specs=[pl.BlockSpec((1, 1, H_D * HD_D), lambda b: (b, 0, 0)),
                  pl.BlockSpec((1, 1, KV_ROW, n16), lambda b: (layer, b, 0, 0)),
                  pl.BlockSpec((None, 3, HD_D, 1), lambda b: (layer, 0, 0, 0)),
                  pl.BlockSpec((n16, n_lane), lambda b: (0, 0))],
        out_specs=[pl.BlockSpec((1, 1, H_D * HD_D), lambda b: (b, 0, 0)),
                   pl.BlockSpec((1, KVH_D, LANES), lambda b: (b, 0, 0))],
        out_shape=[jax.ShapeDtypeStruct((Bd, 1, H_D * HD_D), F32),
                   jax.ShapeDtypeStruct((Bd, KVH_D, LANES), I32)],
        compiler_params=pltpu.CompilerParams(dimension_semantics=("parallel",), vmem_limit_bytes=VMEM_LIMIT),
        name="nsa_dec_select",
    )(nq.reshape(Bd, 1, H_D * HD_D), s16_all, P['kgc'], C['ovl_d'])


def _attend_rows(qs, kt, vt, valid, k_new, v_new):
    s = _dot(qs.astype(BF16), kt.astype(BF16))
    if valid is not None:
        s = jnp.where(valid > 0, s, NEG)
    s_new = jnp.sum(qs * k_new, axis=1, keepdims=True)
    m = jnp.maximum(jnp.max(s, axis=-1, keepdims=True), s_new)
    p = jnp.exp2(s - m)
    if valid is not None:
        p = jnp.where(valid > 0, p, 0.0)
    p_new = jnp.exp2(s_new - m)
    den = jnp.sum(p, axis=-1, keepdims=True) + p_new
    return (_dot_nt(p.astype(BF16), vt.astype(BF16)) + p_new * v_new) / den


def _nsa_dec_attend_kernel(n_past_blocks, k_sel, pt_ref, ix_ref, q_ref, tail_ref, ocmp_ref, slcn_ref, winn_ref,
                           winc_ref, *refs):
    blk_refs = refs[:2 * KVH_D * k_sel]
    o_ref = refs[2 * KVH_D * k_sel]
    b = pl.program_id(0)
    q = q_ref[0]
    gates = tail_ref[0]
    ocmp = ocmp_ref[0]
    slc_new = slcn_ref[0]
    win_new = winn_ref[0]
    per_page = PAGE_SIZE // SEL_BLOCK
    lane_pos = lax.broadcasted_iota(I32, (G_D, k_sel * PAGE_SIZE), 1)
    lane_slot = lane_pos >> PAGE_SHIFT
    lane_sub = (lane_pos >> SEL_SHIFT) & (per_page - 1)
    for kvh in range(KVH_D):
        ksl = slice(kvh * HD_D, (kvh + 1) * HD_D)
        vsl = slice(LANES + kvh * HD_D, LANES + (kvh + 1) * HD_D)
        qs = _stack_heads(q, kvh)
        slab = lambda kv: jnp.concatenate(
            [blk_refs[(kvh * k_sel + r) * 2 + kv][0] for r in range(k_sel)], axis=1)
        kt, vt = slab(0), slab(1)
        valid = jnp.zeros((G_D, k_sel * PAGE_SIZE), I32)
        for r in range(k_sel):
            blk = ix_ref[b, kvh * LANES + r]
            in_past = (blk < n_past_blocks).astype(I32)
            valid = jnp.where((lane_slot == r) & (lane_sub == blk % per_page), in_past, valid)
        o_slc = _attend_rows(qs, kt, vt, valid, slc_new[:, ksl], slc_new[:, vsl])
        o_win = _attend_rows(qs, winc_ref[0, ksl, :], winc_ref[0, vsl, :], None, win_new[:, ksl], win_new[:, vsl])
        for g in range(G_D):
            hd = kvh * G_D + g
            gc = TAIL_GATE0 + hd * 3
            hsl = slice(hd * HD_D, (hd + 1) * HD_D)
            o_ref[0, :, hsl] = (gates[:, gc:gc + 1] * ocmp[:, hsl] + gates[:, gc + 1:gc + 2] * o_slc[g:g + 1, :]
                                + gates[:, gc + 2:gc + 3] * o_win[g:g + 1, :])


def _nsa_dec_attend(nq, tail, ocmp, idx, slc_new, win_new, cache_slc, cache_win, page_table, layer, C):
    Bd = nq.shape[0]
    depth, n_phys = cache_slc.shape[0], cache_slc.shape[1]
    n_past_blocks = page_table.shape[1] * (PAGE_SIZE // SEL_BLOCK)
    k_sel = min(TOPK, C['n_sel_d'])
    per_page = PAGE_SIZE // SEL_BLOCK
    slc_flat = cache_slc.reshape(depth * n_phys * 2 * KVH_D, HD_D, PAGE_SIZE)
    n_win = cache_win.shape[-1]
    win_flat = cache_win.reshape(depth * Bd, KV_ROW, n_win)

    def blk_spec(kvh, r, kv):
        def imap(b, pt, ix):
            blk = jnp.minimum(ix[b, kvh * LANES + r], n_past_blocks - 1)
            return (((layer * n_phys + pt[b, blk // per_page]) * 2 + kv) * KVH_D + kvh, 0, 0)
        return pl.BlockSpec((1, HD_D, PAGE_SIZE), imap)

    one = lambda w: pl.BlockSpec((1, 1, w), lambda b, pt, ix: (b, 0, 0))
    r3 = lambda a: a.reshape(Bd, 1, a.shape[-1])
    o = pl.pallas_call(
        functools.partial(_nsa_dec_attend_kernel, n_past_blocks, k_sel),
        grid_spec=pltpu.PrefetchScalarGridSpec(
            num_scalar_prefetch=2,
            grid=(Bd,),
            in_specs=[one(H_D * HD_D), one(LANES), one(H_D * HD_D), one(KV_ROW), one(KV_ROW),
                      pl.BlockSpec((1, KV_ROW, n_win), lambda b, pt, ix: (layer * Bd + b, 0, 0))]
                     + [blk_spec(kvh, r, kv) for kvh in range(KVH_D) for r in range(k_sel) for kv in range(2)],
            out_specs=one(H_D * HD_D)),
        out_shape=jax.ShapeDtypeStruct((Bd, 1, H_D * HD_D), F32),
        compiler_params=pltpu.CompilerParams(dimension_semantics=("arbitrary",), vmem_limit_bytes=VMEM_LIMIT),
        name="nsa_dec_attend",
    )(page_table, idx.reshape(Bd, KVH_D * LANES), r3(nq), r3(tail), ocmp, r3(slc_new), r3(win_new), win_flat,
      *([slc_flat] * (2 * KVH_D * k_sel)))
    return o.reshape(Bd, H_D * HD_D)


def _merge_kernel(x_ref, b0_ref, b1_ref, b2_ref, b3_ref, n1_ref, wg_ref, bg_ref, wb_ref, wo_ref, o_ref):
    x = x_ref[...]
    D = x.shape[1]
    hb = _rms(x, n1_ref[...]).astype(BF16)
    acc = jnp.zeros(x.shape, F32)
    for n, br in enumerate((b0_ref, b1_ref, b2_ref, b3_ref)):
        gate = _sigmoid(_dot(hb, wg_ref[:, n * D:(n + 1) * D]) + bg_ref[:, n * D:(n + 1) * D])
        acc = acc + gate * _dot(br[...].astype(BF16), wb_ref[n])
    o_ref[...] = x + _dot(acc.astype(BF16), wo_ref[...])


def _merge(x2d, branches, layer, P, tm):
    T, D = x2d.shape
    row = lambda w: pl.BlockSpec((tm, w), lambda i: (i, 0))
    lay2 = lambda a, b: pl.BlockSpec((None, a, b), lambda i: (layer, 0, 0))
    return pl.pallas_call(
        _merge_kernel,
        grid=(T // tm,),
        in_specs=[row(D)] + [row(b.shape[1]) for b in branches]
                 + [lay2(1, D), lay2(D, N_BRANCH * D), lay2(1, N_BRANCH * D),
                    pl.BlockSpec((None, N_BRANCH, D // 2, D), lambda i: (layer, 0, 0, 0)), lay2(D, D)],
        out_specs=row(D),
        out_shape=jax.ShapeDtypeStruct((T, D), F32),
        compiler_params=pltpu.CompilerParams(dimension_semantics=("parallel",), vmem_limit_bytes=VMEM_LIMIT),
        name="merge",
    )(x2d, *branches, P['n1'], P['w_gate'], P['b_gate'], P['w_branch'], P['w_out'])


def _mlp_kernel(x_ref, n2_ref, w1_ref, w2_ref, o_ref):
    x = x_ref[...]
    u = jnp.maximum(_dot(_rms(x, n2_ref[...]).astype(BF16), w1_ref[...]), 0.0)
    o_ref[...] = x + _dot((u * u).astype(BF16), w2_ref[...])


def _mlp(x2d, layer, P, tm):
    T, D = x2d.shape
    dff = P['w1'].shape[2]
    row = pl.BlockSpec((tm, D), lambda i: (i, 0))
    return pl.pallas_call(
        _mlp_kernel,
        grid=(T // tm,),
        in_specs=[row, pl.BlockSpec((None, 1, D), lambda i: (layer, 0, 0)),
                  pl.BlockSpec((None, D, dff), lambda i: (layer, 0, 0)),
                  pl.BlockSpec((None, dff, D), lambda i: (layer, 0, 0))],
        out_specs=row,
        out_shape=jax.ShapeDtypeStruct((T, D), F32),
        compiler_params=pltpu.CompilerParams(dimension_semantics=("parallel",), vmem_limit_bytes=VMEM_LIMIT),
        name="mlp",
    )(x2d, P['n2'], P['w1'], P['w2'])


def _prep_params(norm1_g, w_in, gla_wa2, gla_ba, gla_norm_g, hgrn_lb_param, hgrn_norm_g, pool_w, pool_scale,
                 nsa_q_norm_g, nsa_k_norm_g, w_branch, w_gate, b_gate, w_out, norm2_g, mlp_w1, mlp_w2):
    depth = w_in.shape[0]
    sizes = (QK, QK, VW, VW, GLA_RANK, QK, QK, VW, VW, 512, H_D * HD_D) + (KVH_D * HD_D,) * 6 + (H_D * 3,)
    offs = [0]
    for s in sizes:
        offs.append(offs[-1] + s)
    col = lambda i: w_in[:, :, offs[i]:offs[i + 1]]
    order = [0, 1, 2, 3, 5, 6, 7, 8, 9, 10, 11, 12, 13, 14, 15, 16, 4, 17]
    pad = jnp.zeros(w_in.shape[:2] + (LANES - GLA_RANK - H_D * 3,), w_in.dtype)
    w_in_r = jnp.concatenate([col(i) for i in order] + [pad], axis=-1).astype(BF16)
    wa2 = jnp.concatenate([gla_wa2, jnp.zeros((depth, LANES - GLA_RANK, QK), gla_wa2.dtype)], axis=1).astype(BF16)
    r3 = lambda a: a.reshape(depth, 1, a.shape[-1])
    lane = jnp.arange(LANES)
    seg = jnp.where((lane[:, None] // HD_D) == (lane[None, :] // HD_D), 1.0 / HD_D, 0.0).astype(BF16)
    w_kvt = jnp.transpose(w_in_r[:, :, OFF_KV:OFF_TAIL], (0, 2, 1))
    return dict(
        n1=r3(norm1_g), w_in=w_in_r, w_kvt=w_kvt, wa2=wa2, ba=r3(gla_ba), lbp=hgrn_lb_param,
        qg=r3(jnp.tile(nsa_q_norm_g, (1, H_D))), kg=jnp.tile(nsa_k_norm_g, (1, 1, KVH_D)), seg=seg,
        kgc=nsa_k_norm_g.reshape(depth, 3, HD_D, 1),
        gla_g=gla_norm_g, hgrn_g=hgrn_norm_g, pool_w=pool_w.astype(BF16), pool_sc=r3(pool_scale),
        w_gate=w_gate.astype(BF16), b_gate=r3(b_gate), w_branch=w_branch.astype(BF16), w_out=w_out.astype(BF16),
        n2=r3(norm2_g), w1=mlp_w1.astype(BF16), w2=mlp_w2.astype(BF16))


def _overlap(n_cmp_rows, n_lanes):
    per = SEL_BLOCK // CMP_STRIDE
    ci = jnp.arange(n_cmp_rows)[:, None]
    sj = jnp.arange(n_lanes)[None, :]
    return jnp.where((ci // per == sj) | ((ci + 1) // per == sj), 1.0, 0.0).astype(BF16)


def _prep_consts(L, past):
    TQ = 256
    KC = 512
    kpos = jnp.arange(L).reshape(L // KC, 1, KC)
    exp_p = jnp.where(kpos // SEL_BLOCK == jnp.arange(LANES)[None, :, None], 1.0, 0.0).astype(BF16)
    n16_d = past // CMP_STRIDE
    n_sel_d = (past + 1 + SEL_BLOCK - 1) // SEL_BLOCK
    n_lane_d = -(-n_sel_d // LANES) * LANES
    r = jnp.arange(L)[:, None]
    c0 = jnp.arange(L // CMP_STRIDE)[None, :] * CMP_STRIDE
    pool32_p = jnp.where((r >= c0) & (r < c0 + CMP_BLOCK), 1.0, 0.0).astype(BF16)
    return dict(TQ=TQ, KC=KC, ovl_p=_overlap(L // CMP_STRIDE, L // SEL_BLOCK).T, exp_p=exp_p, pool32_p=pool32_p,
                ovl_d=_overlap(n16_d, n_lane_d), n_sel_d=n_sel_d)


def kernel(x_prompt, x_sample, state_gla, state_hgrn, state_pool, cache_cmp_kv, cache_slc_kv, cache_win_kv,
           page_table, norm1_g, w_in, gla_wa2, gla_ba, gla_norm_g, hgrn_lb_param, hgrn_norm_g, pool_w,
           pool_scale, nsa_q_norm_g, nsa_k_norm_g, w_branch, w_gate, b_gate, w_out, norm2_g, mlp_w1, mlp_w2):
    B, L, D = x_prompt.shape
    Bd = x_sample.shape[0]
    depth = w_in.shape[0]
    n_pages = page_table.shape[1]
    past = n_pages * PAGE_SIZE
    assert x_sample.shape[1] == 1 and L % 512 == 0 and L >= WINDOW + 256 and past % SEL_BLOCK == 0
    assert cache_win_kv.shape[2] == WINDOW and past >= WINDOW

    P = _prep_params(norm1_g, w_in, gla_wa2, gla_ba, gla_norm_g, hgrn_lb_param, hgrn_norm_g, pool_w, pool_scale,
                     nsa_q_norm_g, nsa_k_norm_g, w_branch, w_gate, b_gate, w_out, norm2_g, mlp_w1, mlp_w2)
    C = _prep_consts(L, past)
    tm_p = 256
    cb = 128
    bb = 8 if Bd % 8 == 0 else Bd
    keep = min(WINDOW, L)

    pps = 16 if n_pages % 16 == 0 else n_pages
    cmp_fm = _feature_major(cache_cmp_kv).astype(BF16)
    slc_fm = _feature_major(cache_slc_kv).astype(BF16)
    win_fm = _feature_major(cache_win_kv).astype(BF16)
    s16_all = _page_sums(cmp_fm, page_table, pps)

    xp = x_prompt.reshape(B * L, D)
    xs = x_sample.reshape(Bd, D)
    outs = {k: [] for k in ('gla_p', 'gla_s', 'hgrn_p', 'hgrn_s', 'pool_p', 'pool_s', 'cmp_p', 'cmp_s',
                            'slc_p', 'slc_s', 'win_p', 'win_s')}
    zero_prev = jnp.zeros((B, POOL_HIST + 1, 512), F32)
    rows5 = lambda a, n: a.reshape(n, -1, 2, KVH_D, HD_D)
    for l in range(depth):
        gqkg, gvr, hqkg, hvr, u, nq, cmp_t, slc_c, win_c, tail = _in_proj(xp, l, P, tm_p, seq_len=L, kc=C['KC'])
        b3 = lambda a: a.reshape(B, L, a.shape[-1])
        o_gla, s_gla = _gla_prompt(b3(gqkg), b3(gvr), P['gla_g'][l:l + 1], cb)
        o_hgrn, s_hgrn = _gla_prompt(b3(hqkg), b3(hvr), P['hgrn_g'][l:l + 1], cb)
        o_pool = _pool(b3(u), zero_prev, l, P, 0)
        kcvc = _cmp_prompt(cmp_t, l, P, C)
        o_nsa = _nsa_prompt(b3(nq), b3(tail), kcvc, slc_c, win_c, C)
        flat = lambda a: a.reshape(B * L, a.shape[-1])
        xp = _merge(xp, (flat(o_gla), flat(o_hgrn), flat(o_pool), flat(o_nsa)), l, P, tm_p)
        xp = _mlp(xp, l, P, tm_p)
        outs['gla_p'].append(s_gla)
        outs['hgrn_p'].append(s_hgrn)
        outs['pool_p'].append(b3(u)[:, L - POOL_HIST:])
        unchunk = lambda a: jnp.transpose(a, (0, 2, 1, 3)).reshape(B, KV_ROW, -1)
        outs['cmp_p'].append(cmp_t)
        outs['slc_p'].append(unchunk(slc_c))
        outs['win_p'].append(unchunk(win_c[:, (L - keep) // LANES:]))

        gqkg, gvr, hqkg, hvr, u, nq, cmp_r, slc_r, win_r, tail = _in_proj(xs, l, P, Bd)
        o_gla, s_gla = _gla_decode(gqkg, gvr, P['gla_g'][l], state_gla, l, bb)
        o_hgrn, s_hgrn = _gla_decode(hqkg, hvr, P['hgrn_g'][l], state_hgrn, l, bb)
        prev = state_pool[l]
        prev16 = jnp.concatenate([jnp.zeros((Bd, 1, 512), F32), prev], axis=1)
        o_pool = _pool(u.reshape(Bd, 1, 512), prev16, l, P, past).reshape(Bd, 512)
        ocmp, idx = _nsa_dec_select(nq, s16_all, l, P, C, past)
        o_nsa = _nsa_dec_attend(nq, tail, ocmp, idx, slc_r, win_r, slc_fm, win_fm, page_table, l, C)
        xs = _merge(xs, (o_gla, o_hgrn, o_pool, o_nsa), l, P, Bd)
        xs = _mlp(xs, l, P, Bd)
        outs['gla_s'].append(s_gla)
        outs['hgrn_s'].append(s_hgrn)
        outs['pool_s'].append(jnp.concatenate([prev[:, 1:], u.reshape(Bd, 1, 512)], axis=1))
        outs['cmp_s'].append(rows5(cmp_r, Bd))
        outs['slc_s'].append(rows5(slc_r, Bd))
        outs['win_s'].append(rows5(win_r, Bd))

    st = lambda k: jnp.stack(outs[k])

    def rows_p(k):
        a = st(k)
        return jnp.transpose(a.reshape(depth, B, 2, KVH_D, HD_D, a.shape[-1]), (0, 1, 5, 2, 3, 4))

    return (xp.reshape(B, L, D), xs.reshape(Bd, 1, D),
            st('gla_p'), st('gla_s'), st('hgrn_p'), st('hgrn_s'), st('pool_p'), st('pool_s'),
            rows_p('cmp_p'), st('cmp_s'), rows_p('slc_p'), st('slc_s'), rows_p('win_p'), st('win_s'))
```

```python
import functools

import jax
import jax.numpy as jnp
from jax import lax
from jax.experimental import pallas as pl
from jax.experimental.pallas import tpu as pltpu

F32 = jnp.float32
BF16 = jnp.bfloat16
I32 = jnp.int32

N_BRANCH = 4
H_LIN = 4
DK_LIN = 64
DV_LIN = 128
GLA_RANK = 16
GLA_TAU = 16.0
POOL_WINDOWS = (2, 4, 8, 16)
POOL_HIST = 15
POOL_GROUP = 128
H_D = 8
HD_D = 64
KVH_D = 2
G_D = 4
CMP_STRIDE = 16
CMP_BLOCK = 32
SEL_BLOCK = 64
SEL_SHIFT = 6
TOPK = 16
WINDOW = 512
FORCE_BONUS = 1000.0
PAGE_SIZE = 128
PAGE_SHIFT = 7
EPS = 1e-6
NEG = -1e30
LOG2E = 1.4426950408889634

LANES = 128
VMEM_LIMIT = 56 * 1024 * 1024

QK = H_LIN * DK_LIN
VW = H_LIN * DV_LIN
OFF_GLA = 0
OFF_HGRN = OFF_GLA + 2 * QK + 2 * VW
OFF_POOL = OFF_HGRN + 2 * QK + 2 * VW
OFF_NQ = OFF_POOL + 512
OFF_KV = OFF_NQ + H_D * HD_D
OFF_TAIL = OFF_KV + 6 * KVH_D * HD_D
IN_PAD = OFF_TAIL + LANES
TAIL_GATE0 = GLA_RANK
KV_ROW = 2 * KVH_D * HD_D


def _dot(a, b):
    return jnp.dot(a, b, preferred_element_type=F32)


def _dot_nt(a, b):
    return lax.dot_general(a, b, (((1,), (1,)), ((), ())), preferred_element_type=F32)


def _rms(x, gain):
    return x * lax.rsqrt(jnp.mean(x * x, axis=-1, keepdims=True) + EPS) * gain


def _seg_mean(x2, seg):
    parts = []
    for c in range(x2.shape[1] // LANES):
        xc = x2[:, c * LANES:(c + 1) * LANES]
        h1 = xc.astype(BF16)
        r1 = xc - h1.astype(F32)
        h2 = r1.astype(BF16)
        h3 = (r1 - h2.astype(F32)).astype(BF16)
        parts.append(_dot(h1, seg) + _dot(h2, seg) + _dot(h3, seg))
    return parts[0] if len(parts) == 1 else jnp.concatenate(parts, axis=1)


def _seg_rms(x, gain, seg):
    return x * lax.rsqrt(_seg_mean(x * x, seg) + EPS) * gain


def _sigmoid(x):
    return 1.0 / (1.0 + jnp.exp(-x))


def _in_proj_kernel(layer, kv_t, x_ref, n1_ref, w_ref, wa2_ref, ba_ref, lbp_ref, qg_ref, kg_ref, seg_ref,
                    wkvt_ref, kgc_ref,
                    gqkg_ref, gvr_ref, hqkg_ref, hvr_ref, u_ref, nq_ref, cmp_ref, slc_ref, win_ref, tail_ref):
    x = x_ref[...]
    hb = _rms(x, n1_ref[...]).astype(BF16)
    seg = seg_ref[...]

    def proj(c0, c1):
        return _dot(hb, w_ref[:, c0:c1])

    tail = proj(OFF_TAIL, IN_PAD)
    tail_ref[...] = _sigmoid(tail)

    a_pre = _dot(tail.astype(BF16), wa2_ref[...]) + ba_ref[...]
    log_a = (jnp.minimum(a_pre, 0.0) - jnp.log(1.0 + jnp.exp(-jnp.abs(a_pre)))) / GLA_TAU
    z = proj(OFF_GLA, OFF_GLA + 2 * QK)
    gqkg_ref[:, 0:QK] = z[:, 0:QK] * (DK_LIN ** -0.5)
    gqkg_ref[:, QK:2 * QK] = z[:, QK:2 * QK]
    gqkg_ref[:, 2 * QK:3 * QK] = log_a
    gvr_ref[...] = proj(OFF_GLA + 2 * QK, OFF_HGRN)

    lbp = lbp_ref[...]
    pe = jnp.exp(lbp - jnp.max(lbp, axis=0, keepdims=True))
    soft = pe / jnp.sum(pe, axis=0, keepdims=True)
    lb = jnp.zeros((1, QK), F32)
    for i in range(1, layer + 1):
        lb = lb + soft[i:i + 1, :]
    z = proj(OFF_HGRN, OFF_HGRN + 2 * QK)
    hq = z[:, 0:QK]
    fgate = lb + (1.0 - lb) * _sigmoid(z[:, QK:2 * QK])
    hqkg_ref[:, 0:QK] = hq * _sigmoid(hq) * (DK_LIN ** -0.5)
    hqkg_ref[:, QK:2 * QK] = 1.0 - fgate
    hqkg_ref[:, 2 * QK:3 * QK] = jnp.log(fgate)
    hvr_ref[...] = proj(OFF_HGRN + 2 * QK, OFF_POOL)

    u_ref[...] = proj(OFF_POOL, OFF_NQ)
    nq_ref[...] = _seg_rms(proj(OFF_NQ, OFF_KV), qg_ref[...], seg)

    if kv_t:
        zt = _dot_nt(wkvt_ref[...], hb)
        cmp_ref[0] = zt[0:KV_ROW]

        def normed_rows(base, gi):
            ks = [_norm_feature_major(zt[base + kvh * HD_D:base + (kvh + 1) * HD_D], kgc_ref[gi])
                  for kvh in range(KVH_D)]
            return jnp.concatenate(ks + [zt[base + LANES:base + KV_ROW]], axis=0)

        slc_ref[0, 0] = normed_rows(KV_ROW, 1)
        win = normed_rows(2 * KV_ROW, 2)
        for c in range(win_ref.shape[1]):
            win_ref[0, c] = win[:, c * LANES:(c + 1) * LANES]
    else:
        zkv = proj(OFF_KV, OFF_TAIL)
        cmp_ref[...] = zkv[:, 0:KV_ROW]
        slc_ref[:, 0:LANES] = _seg_rms(zkv[:, 256:384], kg_ref[1:2, :], seg)
        slc_ref[:, LANES:KV_ROW] = zkv[:, 384:512]
        win_ref[:, 0:LANES] = _seg_rms(zkv[:, 512:640], kg_ref[2:3, :], seg)
        win_ref[:, LANES:KV_ROW] = zkv[:, 640:768]


def _in_proj(x2d, layer, P, tm, seq_len=None, kc=None):
    T, D = x2d.shape
    depth = P['lbp'].shape[0]
    kv_t = seq_len is not None
    row = lambda w: pl.BlockSpec((tm, w), lambda i: (i, 0))
    lay2 = lambda a, b: pl.BlockSpec((None, a, b), lambda i: (layer, 0, 0))
    widths = (3 * QK, 2 * VW, 3 * QK, 2 * VW, 512, H_D * HD_D)
    out_specs = [row(w) for w in widths]
    out_shape = [jax.ShapeDtypeStruct((T, w), F32) for w in widths]
    if kv_t:
        nt = seq_len // tm
        per = kc // tm
        nb = T // seq_len
        out_specs += [pl.BlockSpec((1, KV_ROW, tm), lambda i: (i // nt, 0, i % nt)),
                      pl.BlockSpec((1, 1, KV_ROW, tm), lambda i: (i // nt, (i % nt) // per, 0, (i % nt) % per)),
                      pl.BlockSpec((1, tm // LANES, KV_ROW, LANES), lambda i: (i // nt, i % nt, 0, 0))]
        out_shape += [jax.ShapeDtypeStruct((nb, KV_ROW, seq_len), F32),
                      jax.ShapeDtypeStruct((nb, seq_len // kc, KV_ROW, kc), F32),
                      jax.ShapeDtypeStruct((nb, seq_len // LANES, KV_ROW, LANES), F32)]
    else:
        out_specs += [row(KV_ROW)] * 3
        out_shape += [jax.ShapeDtypeStruct((T, KV_ROW), F32)] * 3
    out_specs.append(row(LANES))
    out_shape.append(jax.ShapeDtypeStruct((T, LANES), F32))
    return pl.pallas_call(
        functools.partial(_in_proj_kernel, layer, kv_t),
        grid=(T // tm,),
        in_specs=[row(D), lay2(1, D), lay2(D, IN_PAD), lay2(LANES, QK), lay2(1, QK),
                  pl.BlockSpec((depth, QK), lambda i: (0, 0)), lay2(1, H_D * HD_D), lay2(3, LANES),
                  pl.BlockSpec((LANES, LANES), lambda i: (0, 0)), lay2(3 * KV_ROW, D),
                  pl.BlockSpec((None, 3, HD_D, 1), lambda i: (layer, 0, 0, 0))],
        out_specs=out_specs,
        out_shape=out_shape,
        compiler_params=pltpu.CompilerParams(dimension_semantics=("parallel",), vmem_limit_bytes=VMEM_LIMIT),
        name="in_proj",
    )(x2d, P['n1'], P['w_in'], P['wa2'], P['ba'], P['lbp'], P['qg'], P['kg'], P['seg'], P['w_kvt'], P['kgc'])


def _gla_kernel(levels, qkg_ref, vr_ref, gain_ref, o_ref, s_ref):
    CB = qkg_ref.shape[1]

    @pl.when(pl.program_id(1) == 0)
    def _():
        s_ref[...] = jnp.zeros_like(s_ref)

    q = qkg_ref[0, :, 0:QK]
    k = qkg_ref[0, :, QK:2 * QK]
    g = qkg_ref[0, :, 2 * QK:3 * QK]
    row = lax.broadcasted_iota(I32, (CB, 1), 0)
    r2 = lax.broadcasted_iota(I32, (CB, CB), 0)
    c2 = lax.broadcasted_iota(I32, (CB, CB), 1)
    heads = [slice(h * DK_LIN, (h + 1) * DK_LIN) for h in range(H_LIN)]

    qb = q.astype(BF16)
    kb = k.astype(BF16)
    att = [jnp.where(r2 == c2, _dot_nt(qb[:, hs], kb[:, hs]), 0.0) for hs in heads]
    c = g
    tot = g
    for lvl in range(levels):
        half = 1 << lvl
        upper = ((row >> lvl) & 1) == 1
        qt = jnp.where(upper, q * jnp.exp(c), 0.0).astype(BF16)
        kt = jnp.where(upper, 0.0, k * jnp.exp(tot - c)).astype(BF16)
        same = (r2 >> (lvl + 1)) == (c2 >> (lvl + 1))
        for h, hs in enumerate(heads):
            att[h] = att[h] + jnp.where(same, _dot_nt(qt[:, hs], kt[:, hs]), 0.0)
        t_lo = pltpu.roll(tot, half, 0)
        t_hi = pltpu.roll(tot, CB - half, 0)
        c = c + jnp.where(upper, t_lo, 0.0)
        tot = tot + jnp.where(upper, t_lo, t_hi)

    qd = (q * jnp.exp(c)).astype(BF16)
    kdt = jnp.transpose(k * jnp.exp(tot - c)).astype(BF16)
    eye = (lax.broadcasted_iota(I32, (DK_LIN, DK_LIN), 0) == lax.broadcasted_iota(I32, (DK_LIN, DK_LIN), 1))
    gain = gain_ref[...]
    for h, hs in enumerate(heads):
        vs = slice(h * DV_LIN, (h + 1) * DV_LIN)
        v = vr_ref[0, :, vs].astype(BF16)
        s_old = s_ref[0, h]
        o = _dot(att[h].astype(BF16), v) + _dot(qd[:, hs], s_old.astype(BF16))
        t_row = jnp.broadcast_to(tot[0:1, hs], (DK_LIN, DK_LIN))
        t_col = jnp.sum(jnp.where(eye, t_row, 0.0), axis=1, keepdims=True)
        s_ref[0, h] = s_old * jnp.exp(t_col) + _dot(kdt[hs, :], v)
        r = vr_ref[0, :, VW + h * DV_LIN:VW + (h + 1) * DV_LIN]
        o_ref[0, :, vs] = _rms(o, gain) * (r * _sigmoid(r))


def _gla_prompt(qkg, vr, gain, cb):
    B, L, _ = qkg.shape
    levels = cb.bit_length() - 1
    return pl.pallas_call(
        functools.partial(_gla_kernel, levels),
        grid=(B, L // cb),
        in_specs=[pl.BlockSpec((1, cb, 3 * QK), lambda b, j: (b, j, 0)),
                  pl.BlockSpec((1, cb, 2 * VW), lambda b, j: (b, j, 0)),
                  pl.BlockSpec((1, DV_LIN), lambda b, j: (0, 0))],
        out_specs=[pl.BlockSpec((1, cb, VW), lambda b, j: (b, j, 0)),
                   pl.BlockSpec((1, H_LIN, DK_LIN, DV_LIN), lambda b, j: (b, 0, 0, 0))],
        out_shape=[jax.ShapeDtypeStruct((B, L, VW), F32),
                   jax.ShapeDtypeStruct((B, H_LIN, DK_LIN, DV_LIN), F32)],
        compiler_params=pltpu.CompilerParams(dimension_semantics=("parallel", "arbitrary"),
                                             vmem_limit_bytes=VMEM_LIMIT),
        name="gla_prompt",
    )(qkg, vr, gain)


def _gla_dec_kernel(q_ref, k_ref, g_ref, v_ref, r_ref, gain_ref, s0_ref, o_ref, s_ref):
    s_new = s0_ref[...] * jnp.exp(g_ref[...]) + k_ref[...] * v_ref[...]
    s_ref[...] = s_new
    o = jnp.sum(q_ref[...] * s_new, axis=2, keepdims=True)
    r = r_ref[...]
    o_ref[...] = _rms(o, gain_ref[...]) * (r * _sigmoid(r))


def _gla_decode(qkg, vr, gain, state, layer, bb):
    Bd = qkg.shape[0]
    col = lambda a: a.reshape(Bd, H_LIN, DK_LIN, 1)
    rowv = lambda a: a.reshape(Bd, H_LIN, 1, DV_LIN)
    q, k, g = col(qkg[:, 0:QK]), col(qkg[:, QK:2 * QK]), col(qkg[:, 2 * QK:])
    v, r = rowv(vr[:, 0:VW]), rowv(vr[:, VW:])
    cspec = pl.BlockSpec((bb, H_LIN, DK_LIN, 1), lambda i: (i, 0, 0, 0))
    rspec = pl.BlockSpec((bb, H_LIN, 1, DV_LIN), lambda i: (i, 0, 0, 0))
    o, s_new = pl.pallas_call(
        _gla_dec_kernel,
        grid=(Bd // bb,),
        in_specs=[cspec, cspec, cspec, rspec, rspec,
                  pl.BlockSpec((1, 1, 1, DV_LIN), lambda i: (0, 0, 0, 0)),
                  pl.BlockSpec((None, bb, H_LIN, DK_LIN, DV_LIN), lambda i: (layer, i, 0, 0, 0))],
        out_specs=[rspec, pl.BlockSpec((bb, H_LIN, DK_LIN, DV_LIN), lambda i: (i, 0, 0, 0))],
        out_shape=[jax.ShapeDtypeStruct((Bd, H_LIN, 1, DV_LIN), F32),
                   jax.ShapeDtypeStruct((Bd, H_LIN, DK_LIN, DV_LIN), F32)],
        compiler_params=pltpu.CompilerParams(dimension_semantics=("parallel",), vmem_limit_bytes=VMEM_LIMIT),
        name="gla_decode",
    )(q, k, g, v, r, gain.reshape(1, 1, 1, DV_LIN), state)
    return o.reshape(Bd, VW), s_new


def _pool_kernel(pos0, rt, u_ref, prev_ref, w_ref, sc_ref, o_ref, ext_ref):
    L = u_ref.shape[1]
    hdr = POOL_HIST + 1
    ext_ref[0:hdr, :] = prev_ref[0]
    ext_ref[hdr:hdr + L, :] = u_ref[0]
    for r0 in range(0, L, rt):
        n_avail = (pos0 + r0 + 1 + lax.broadcasted_iota(I32, (rt, 1), 0)).astype(F32)
        for gi, w in enumerate(POOL_WINDOWS):
            cs = slice(gi * POOL_GROUP, (gi + 1) * POOL_GROUP)
            tok = ext_ref[hdr + r0:hdr + r0 + rt, cs]
            acc = tok
            for j in range(1, w):
                acc = acc + ext_ref[hdr + r0 - j:hdr + r0 - j + rt, cs]
            d = acc / jnp.minimum(float(w), n_avail) - tok
            o_ref[0, r0:r0 + rt, cs] = _dot(d.astype(BF16), w_ref[gi]) * sc_ref[:, cs]


def _pool(u, prev16, layer, P, pos0):
    B, L, C = u.shape
    rt = min(L, 256)
    return pl.pallas_call(
        functools.partial(_pool_kernel, pos0, rt),
        grid=(B,),
        in_specs=[pl.BlockSpec((1, L, C), lambda b: (b, 0, 0)),
                  pl.BlockSpec((1, POOL_HIST + 1, C), lambda b: (b, 0, 0)),
                  pl.BlockSpec((None, len(POOL_WINDOWS), POOL_GROUP, POOL_GROUP), lambda b: (layer, 0, 0, 0)),
                  pl.BlockSpec((None, 1, C), lambda b: (layer, 0, 0))],
        out_specs=pl.BlockSpec((1, L, C), lambda b: (b, 0, 0)),
        out_shape=jax.ShapeDtypeStruct((B, L, C), F32),
        scratch_shapes=[pltpu.VMEM((POOL_HIST + 1 + L, C), F32)],
        compiler_params=pltpu.CompilerParams(dimension_semantics=("parallel",), vmem_limit_bytes=VMEM_LIMIT),
        name="pool",
    )(u, prev16, P['pool_w'], P['pool_sc'])


def _pool_lanes(x, pm):
    hi = x.astype(BF16)
    lo = (x - hi.astype(F32)).astype(BF16)
    return _dot(hi, pm) + _dot(lo, pm)


def _norm_feature_major(x, gain_col):
    return x * lax.rsqrt(jnp.mean(x * x, axis=0, keepdims=True) + EPS) * gain_col


def _cmp_store(cmp, gain_col, o_ref):
    for kvh in range(KVH_D):
        rows = slice(kvh * HD_D, (kvh + 1) * HD_D)
        o_ref[0, rows, :] = _norm_feature_major(cmp[rows], gain_col)
    o_ref[0, LANES:KV_ROW, :] = cmp[LANES:KV_ROW]


def _cmp_prompt_kernel(rows_ref, kgc_ref, pm_ref, o_ref):
    _cmp_store(_pool_lanes(rows_ref[0], pm_ref[...]) / float(CMP_BLOCK), kgc_ref[0], o_ref)


def _cmp_prompt(rows_t, layer, P, C):
    B, _, L = rows_t.shape
    n16 = L // CMP_STRIDE
    return pl.pallas_call(
        _cmp_prompt_kernel,
        grid=(B,),
        in_specs=[pl.BlockSpec((1, KV_ROW, L), lambda b: (b, 0, 0)),
                  pl.BlockSpec((None, 3, HD_D, 1), lambda b: (layer, 0, 0, 0)),
                  pl.BlockSpec((L, n16), lambda b: (0, 0))],
        out_specs=pl.BlockSpec((1, KV_ROW, n16), lambda b: (b, 0, 0)),
        out_shape=jax.ShapeDtypeStruct((B, KV_ROW, n16), F32),
        compiler_params=pltpu.CompilerParams(dimension_semantics=("parallel",), vmem_limit_bytes=VMEM_LIMIT),
        name="cmp_prompt",
    )(rows_t, P['kgc'], C['pool32_p'])


def _feature_major(cache):
    nd = cache.ndim
    return jnp.transpose(cache, tuple(range(nd - 4)) + (nd - 3, nd - 2, nd - 1, nd - 4))


def _page_sum_kernel(n_in, pt_ref, *refs):
    pm_ref, o_ref = refs[n_in], refs[n_in + 1]
    x = jnp.concatenate([refs[i][0] for i in range(n_in)], axis=1)
    o_ref[0, 0] = _dot(x, pm_ref[...])


def _page_sums(cache, page_table, pages_per_step):
    depth, n_phys = cache.shape[0], cache.shape[1]
    Bd, n_pages = page_table.shape
    flat = cache.reshape(depth * n_phys, KV_ROW, PAGE_SIZE)
    per = PAGE_SIZE // CMP_STRIDE
    n_in = pages_per_step
    pos = jnp.arange(n_in * PAGE_SIZE)
    pm = jnp.where(pos[:, None] // CMP_STRIDE == jnp.arange(n_in * per)[None, :], 1.0, 0.0).astype(BF16)

    def page_spec(i):
        return pl.BlockSpec((1, KV_ROW, PAGE_SIZE),
                            lambda l, b, p, pt: (l * n_phys + pt[b, p * n_in + i], 0, 0))

    return pl.pallas_call(
        functools.partial(_page_sum_kernel, n_in),
        grid_spec=pltpu.PrefetchScalarGridSpec(
            num_scalar_prefetch=1,
            grid=(depth, Bd, n_pages // n_in),
            in_specs=[page_spec(i) for i in range(n_in)]
                     + [pl.BlockSpec((n_in * PAGE_SIZE, n_in * per), lambda l, b, p, pt: (0, 0))],
            out_specs=pl.BlockSpec((1, 1, KV_ROW, n_in * per), lambda l, b, p, pt: (l, b, 0, p))),
        out_shape=jax.ShapeDtypeStruct((depth, Bd, KV_ROW, n_pages * per), F32),
        compiler_params=pltpu.CompilerParams(dimension_semantics=("parallel", "parallel", "arbitrary"),
                                             vmem_limit_bytes=VMEM_LIMIT),
        name="page_sums",
    )(page_table, *([flat] * n_in), pm)


def _rank_count(score, idx, n_blocks, axis):
    cnt = jnp.zeros(score.shape, F32)
    for m in range(n_blocks):
        other = score[m:m + 1, :] if axis == 0 else score[:, m:m + 1]
        beats = jnp.where(other > score, 1.0, jnp.where(other == score, jnp.where(idx > m, 1.0, 0.0), 0.0))
        cnt = cnt + beats
    return cnt


def _softmax2_rows(s2, mask):
    sm = jnp.where(mask, s2, NEG)
    p = jnp.where(mask, jnp.exp2(sm - jnp.max(sm, axis=-1, keepdims=True)), 0.0)
    den = jnp.sum(p, axis=-1, keepdims=True)
    return p / jnp.where(den > 0.0, den, 1.0)


def _nsa_prompt_kernel(TQ, KC, n_sel, q_ref, tail_ref, kcvc_ref, slc_ref, win_ref, ovl_ref, exp_ref, o_ref):
    t0 = pl.program_id(1) * TQ
    R = G_D * TQ
    t = t0 + lax.broadcasted_iota(I32, (TQ, 1), 0)
    t4 = t0 + (lax.broadcasted_iota(I32, (R, 1), 0) & (TQ - 1))
    rep = lambda a: jnp.concatenate([a] * G_D, axis=0)
    q = q_ref[0]
    gates = tail_ref[0]
    n_cmp = kcvc_ref.shape[2]
    lane_c = lax.broadcasted_iota(I32, (1, n_cmp), 1)
    cmask4 = (lane_c * CMP_STRIDE + (CMP_BLOCK - 1)) <= t4
    blk = lax.broadcasted_iota(I32, (n_sel, 1), 0)
    cur = (t0 + lax.broadcasted_iota(I32, (1, TQ), 1)) >> SEL_SHIFT
    j_last = t0 // KC
    k_sel = min(TOPK, n_sel)
    add_rows = lambda s, bias: (s.reshape(G_D, TQ, s.shape[1]) + bias[None]).reshape(R, s.shape[1])
    lane_k = lax.broadcasted_iota(I32, (1, KC), 1)
    WK = WINDOW + TQ
    ws = jnp.maximum(t0 - WINDOW, 0)
    wc = ws // LANES
    dist = t - (ws + lax.broadcasted_iota(I32, (1, WK), 1))
    bias_w = jnp.where((dist >= 0) & (dist <= WINDOW), 0.0, NEG)

    for kvh in range(KVH_D):
        ksl = slice(kvh * HD_D, (kvh + 1) * HD_D)
        vsl = slice(LANES + kvh * HD_D, LANES + (kvh + 1) * HD_D)
        qs = (jnp.concatenate([q[:, (kvh * G_D + g) * HD_D:(kvh * G_D + g + 1) * HD_D] for g in range(G_D)],
                              axis=0) * (HD_D ** -0.5 * LOG2E)).astype(BF16)
        pcf = _softmax2_rows(_dot(qs, kcvc_ref[0, ksl, :].astype(BF16)), cmask4)
        o_cmp = _dot_nt(pcf.astype(BF16), kcvc_ref[0, vsl, :].astype(BF16))
        pcs = pcf[0:TQ]
        for g in range(1, G_D):
            pcs = pcs + pcf[g * TQ:(g + 1) * TQ]
        imp = _dot_nt(ovl_ref[...], pcs.astype(BF16))
        forced = (blk == 0) | (blk == cur) | (blk == cur - 1)
        score = jnp.where(blk <= cur, imp + jnp.where(forced, FORCE_BONUS, 0.0), -jnp.inf)
        cnt = _rank_count(score, blk, n_sel, 0)
        selb = jnp.where(cnt < k_sel, 0.0, NEG)
        selb = jnp.transpose(jnp.concatenate([selb, jnp.zeros((LANES - n_sel, TQ), F32)], axis=0))
        q_aug = jnp.concatenate([rep(selb.astype(BF16)), qs], axis=1)

        def chunk(j, carry, q_aug=q_aug, ksl=ksl, vsl=vsl):
            m_i, l_i, acc = carry
            k_aug = jnp.concatenate([exp_ref[j], slc_ref[0, j, ksl, :].astype(BF16)], axis=0)
            causal = jnp.where(j * KC + lane_k <= t, 0.0, NEG)
            s = add_rows(_dot(q_aug, k_aug), causal)
            m_new = jnp.maximum(m_i, jnp.max(s, axis=-1, keepdims=True))
            alpha = jnp.exp2(m_i - m_new)
            p = jnp.exp2(s - m_new)
            l_new = alpha * l_i + jnp.sum(p, axis=-1, keepdims=True)
            acc = alpha * acc + _dot_nt(p.astype(BF16), slc_ref[0, j, vsl, :].astype(BF16))
            return m_new, l_new, acc

        init = (jnp.full((R, 1), NEG, F32), jnp.zeros((R, 1), F32), jnp.zeros((R, HD_D), F32))
        _, l_f, acc_f = lax.fori_loop(0, j_last + 1, chunk, init)
        o_slc = acc_f / l_f

        win_rows = lambda sl: jnp.concatenate(
            [win_ref[0, wc + i, sl, :].astype(BF16) for i in range(WK // LANES)], axis=1)
        sw = add_rows(_dot(qs, win_rows(ksl)), bias_w)
        pw = jnp.exp2(sw - jnp.max(sw, axis=-1, keepdims=True))
        o_win = _dot_nt(pw.astype(BF16), win_rows(vsl)) / jnp.sum(pw, axis=-1, keepdims=True)

        for g in range(G_D):
            hd = kvh * G_D + g
            rows = slice(g * TQ, (g + 1) * TQ)
            gc = TAIL_GATE0 + hd * 3
            o_ref[0, :, hd * HD_D:(hd + 1) * HD_D] = (gates[:, gc:gc + 1] * o_cmp[rows]
                                                     + gates[:, gc + 1:gc + 2] * o_slc[rows]
                                                     + gates[:, gc + 2:gc + 3] * o_win[rows])


def _nsa_prompt(nq, tail, kcvc, slc, win, C):
    B, L, _ = nq.shape
    TQ, KC = C['TQ'], C['KC']
    n_sel = L // SEL_BLOCK
    n16 = L // CMP_STRIDE
    qspec = lambda w: pl.BlockSpec((1, TQ, w), lambda b, i: (b, i, 0))
    full = lambda n: pl.BlockSpec((1, KV_ROW, n), lambda b, i: (b, 0, 0))
    return pl.pallas_call(
        functools.partial(_nsa_prompt_kernel, TQ, KC, n_sel),
        grid=(B, L // TQ),
        in_specs=[qspec(H_D * HD_D), qspec(LANES), full(n16),
                  pl.BlockSpec((1, L // KC, KV_ROW, KC), lambda b, i: (b, 0, 0, 0)),
                  pl.BlockSpec((1, L // LANES, KV_ROW, LANES), lambda b, i: (b, 0, 0, 0)),
                  pl.BlockSpec((n_sel, n16), lambda b, i: (0, 0)),
                  pl.BlockSpec((L // KC, LANES, KC), lambda b, i: (0, 0, 0))],
        out_specs=qspec(H_D * HD_D),
        out_shape=jax.ShapeDtypeStruct((B, L, H_D * HD_D), F32),
        compiler_params=pltpu.CompilerParams(dimension_semantics=("parallel", "arbitrary"),
                                             vmem_limit_bytes=VMEM_LIMIT),
        name="nsa_prompt",
    )(nq, tail, kcvc, slc, win, C['ovl_p'], C['exp_p'])


def _stack_heads(qrow, kvh):
    return jnp.concatenate([qrow[:, (kvh * G_D + g) * HD_D:(kvh * G_D + g + 1) * HD_D] for g in range(G_D)],
                           axis=0) * (HD_D ** -0.5 * LOG2E)


def _nsa_dec_select_kernel(t_now, n_sel, q_ref, s16_ref, kgc_ref, ovl_ref, ocmp_ref, idx_ref):
    s16 = s16_ref[0, 0]
    n_cmp = s16.shape[1]
    nxt = jnp.where(lax.broadcasted_iota(I32, (1, n_cmp), 1) == n_cmp - 1, 0.0, pltpu.roll(s16, n_cmp - 1, 1))
    cmp = (s16 + nxt) / float(CMP_BLOCK)
    lane_c = lax.broadcasted_iota(I32, (G_D, n_cmp), 1)
    cmask = (lane_c * CMP_STRIDE + (CMP_BLOCK - 1)) <= t_now
    n_lane = ovl_ref.shape[1]
    lane = lax.broadcasted_iota(I32, (1, n_lane), 1)
    cur = t_now // SEL_BLOCK
    k_sel = min(TOPK, n_sel)
    q = q_ref[0]
    for kvh in range(KVH_D):
        qs = _stack_heads(q, kvh).astype(BF16)
        kc = _norm_feature_major(cmp[kvh * HD_D:(kvh + 1) * HD_D], kgc_ref[0])
        pc = _softmax2_rows(_dot(qs, kc.astype(BF16)), cmask).astype(BF16)
        o_cmp = _dot_nt(pc, cmp[LANES + kvh * HD_D:LANES + (kvh + 1) * HD_D].astype(BF16))
        for g in range(G_D):
            hd = kvh * G_D + g
            ocmp_ref[0, :, hd * HD_D:(hd + 1) * HD_D] = o_cmp[g:g + 1, :]
        imp = jnp.sum(_dot(pc, ovl_ref[...]), axis=0, keepdims=True)
        forced = (lane == 0) | (lane == cur) | (lane == cur - 1)
        score = jnp.where(lane <= cur, imp + jnp.where(forced, FORCE_BONUS, 0.0), -jnp.inf)
        cnt = _rank_count(score, lane, n_sel, 1)
        lane_f = lane.astype(F32)
        row = jnp.zeros((1, LANES), F32)
        out_lane = lax.broadcasted_iota(I32, (1, LANES), 1)
        for r in range(k_sel):
            pick = jnp.sum(jnp.where(cnt == float(r), lane_f, 0.0), axis=1, keepdims=True)
            row = jnp.where(out_lane == r, pick, row)
        idx_ref[0, kvh:kvh + 1, :] = row.astype(I32)


def _nsa_dec_select(nq, s16_all, layer, P, C, t_now):
    Bd = nq.shape[0]
    n16 = s16_all.shape[3]
    n_sel = C['n_sel_d']
    n_lane = C['ovl_d'].shape[1]
    return pl.pallas_call(
        functools.partial(_nsa_dec_select_kernel, t_now, n_sel),
        grid=(Bd,),
        in_specs=[pl.BlockSpec((1, 1, H_D * HD_D), lambda b: (b, 0, 0)),
                  pl.BlockSpec((1, 1, KV_ROW, n16), lambda b: (layer, b, 0, 0)),
                  pl.BlockSpec((None, 3, HD_D, 1), lambda b: (layer, 0, 0, 0)),
                  pl.BlockSpec((n16, n_lane), lambda b: (0, 0))],
        out_specs=[pl.BlockSpec((1, 1, H_D * HD_D), lambda b: (b, 0, 0)),
                   pl.BlockSpec((1, KVH_D, LANES), lambda b: (b, 0, 0))],
        out_shape=[jax.ShapeDtypeStruct((Bd, 1, H_D * HD_D), F32),
                   jax.ShapeDtypeStruct((Bd, KVH_D, LANES), I32)],
        compiler_params=pltpu.CompilerParams(dimension_semantics=("parallel",), vmem_limit_bytes=VMEM_LIMIT),
        name="nsa_dec_select",
    )(nq.reshape(Bd, 1, H_D * HD_D), s16_all, P['kgc'], C['ovl_d'])


def _attend_rows(qs, kt, vt, valid, k_new, v_new):
    s = _dot(qs.astype(BF16), kt.astype(BF16))
    if valid is not None:
        s = jnp.where(valid > 0, s, NEG)
    s_new = jnp.sum(qs * k_new, axis=1, keepdims=True)
    m = jnp.maximum(jnp.max(s, axis=-1, keepdims=True), s_new)
    p = jnp.exp2(s - m)
    if valid is not None:
        p = jnp.where(valid > 0, p, 0.0)
    p_new = jnp.exp2(s_new - m)
    den = jnp.sum(p, axis=-1, keepdims=True) + p_new
    return (_dot_nt(p.astype(BF16), vt.astype(BF16)) + p_new * v_new) / den


def _nsa_dec_attend_kernel(n_past_blocks, k_sel, pt_ref, ix_ref, q_ref, tail_ref, ocmp_ref, slcn_ref, winn_ref,
                           winc_ref, *refs):
    blk_refs = refs[:2 * KVH_D * k_sel]
    o_ref = refs[2 * KVH_D * k_sel]
    b = pl.program_id(0)
    q = q_ref[0]
    gates = tail_ref[0]
    ocmp = ocmp_ref[0]
    slc_new = slcn_ref[0]
    win_new = winn_ref[0]
    per_page = PAGE_SIZE // SEL_BLOCK
    lane_pos = lax.broadcasted_iota(I32, (G_D, k_sel * PAGE_SIZE), 1)
    lane_slot = lane_pos >> PAGE_SHIFT
    lane_sub = (lane_pos >> SEL_SHIFT) & (per_page - 1)
    for kvh in range(KVH_D):
        ksl = slice(kvh * HD_D, (kvh + 1) * HD_D)
        vsl = slice(LANES + kvh * HD_D, LANES + (kvh + 1) * HD_D)
        qs = _stack_heads(q, kvh)
        slab = lambda kv: jnp.concatenate(
            [blk_refs[(kvh * k_sel + r) * 2 + kv][0] for r in range(k_sel)], axis=1)
        kt, vt = slab(0), slab(1)
        valid = jnp.zeros((G_D, k_sel * PAGE_SIZE), I32)
        for r in range(k_sel):
            blk = ix_ref[b, kvh * LANES + r]
            in_past = (blk < n_past_blocks).astype(I32)
            valid = jnp.where((lane_slot == r) & (lane_sub == blk % per_page), in_past, valid)
        o_slc = _attend_rows(qs, kt, vt, valid, slc_new[:, ksl], slc_new[:, vsl])
        o_win = _attend_rows(qs, winc_ref[0, ksl, :], winc_ref[0, vsl, :], None, win_new[:, ksl], win_new[:, vsl])
        for g in range(G_D):
            hd = kvh * G_D + g
            gc = TAIL_GATE0 + hd * 3
            hsl = slice(hd * HD_D, (hd + 1) * HD_D)
            o_ref[0, :, hsl] = (gates[:, gc:gc + 1] * ocmp[:, hsl] + gates[:, gc + 1:gc + 2] * o_slc[g:g + 1, :]
                                + gates[:, gc + 2:gc + 3] * o_win[g:g + 1, :])


def _nsa_dec_attend(nq, tail, ocmp, idx, slc_new, win_new, cache_slc, cache_win, page_table, layer, C):
    Bd = nq.shape[0]
    depth, n_phys = cache_slc.shape[0], cache_slc.shape[1]
    n_past_blocks = page_table.shape[1] * (PAGE_SIZE // SEL_BLOCK)
    k_sel = min(TOPK, C['n_sel_d'])
    per_page = PAGE_SIZE // SEL_BLOCK
    slc_flat = cache_slc.reshape(depth * n_phys * 2 * KVH_D, HD_D, PAGE_SIZE)
    n_win = cache_win.shape[-1]
    win_flat = cache_win.reshape(depth * Bd, KV_ROW, n_win)

    def blk_spec(kvh, r, kv):
        def imap(b, pt, ix):
            blk = jnp.minimum(ix[b, kvh * LANES + r], n_past_blocks - 1)
            return (((layer * n_phys + pt[b, blk // per_page]) * 2 + kv) * KVH_D + kvh, 0, 0)
        return pl.BlockSpec((1, HD_D, PAGE_SIZE), imap)

    one = lambda w: pl.BlockSpec((1, 1, w), lambda b, pt, ix: (b, 0, 0))
    r3 = lambda a: a.reshape(Bd, 1, a.shape[-1])
    o = pl.pallas_call(
        functools.partial(_nsa_dec_attend_kernel, n_past_blocks, k_sel),
        grid_spec=pltpu.PrefetchScalarGridSpec(
            num_scalar_prefetch=2,
            grid=(Bd,),
            in_specs=[one(H_D * HD_D), one(LANES), one(H_D * HD_D), one(KV_ROW), one(KV_ROW),
                      pl.BlockSpec((1, KV_ROW, n_win), lambda b, pt, ix: (layer * Bd + b, 0, 0))]
                     + [blk_spec(kvh, r, kv) for kvh in range(KVH_D) for r in range(k_sel) for kv in range(2)],
            out_specs=one(H_D * HD_D)),
        out_shape=jax.ShapeDtypeStruct((Bd, 1, H_D * HD_D), F32),
        compiler_params=pltpu.CompilerParams(dimension_semantics=("arbitrary",), vmem_limit_bytes=VMEM_LIMIT),
        name="nsa_dec_attend",
    )(page_table, idx.reshape(Bd, KVH_D * LANES), r3(nq), r3(tail), ocmp, r3(slc_new), r3(win_new), win_flat,
      *([slc_flat] * (2 * KVH_D * k_sel)))
    return o.reshape(Bd, H_D * HD_D)


def _merge_kernel(x_ref, b0_ref, b1_ref, b2_ref, b3_ref, n1_ref, wg_ref, bg_ref, wb_ref, wo_ref, o_ref):
    x = x_ref[...]
    D = x.shape[1]
    hb = _rms(x, n1_ref[...]).astype(BF16)
    acc = jnp.zeros(x.shape, F32)
    for n, br in enumerate((b0_ref, b1_ref, b2_ref, b3_ref)):
        gate = _sigmoid(_dot(hb, wg_ref[:, n * D:(n + 1) * D]) + bg_ref[:, n * D:(n + 1) * D])
        acc = acc + gate * _dot(br[...].astype(BF16), wb_ref[n])
    o_ref[...] = x + _dot(acc.astype(BF16), wo_ref[...])


def _merge(x2d, branches, layer, P, tm):
    T, D = x2d.shape
    row = lambda w: pl.BlockSpec((tm, w), lambda i: (i, 0))
    lay2 = lambda a, b: pl.BlockSpec((None, a, b), lambda i: (layer, 0, 0))
    return pl.pallas_call(
        _merge_kernel,
        grid=(T // tm,),
        in_specs=[row(D)] + [row(b.shape[1]) for b in branches]
                 + [lay2(1, D), lay2(D, N_BRANCH * D), lay2(1, N_BRANCH * D),
                    pl.BlockSpec((None, N_BRANCH, D // 2, D), lambda i: (layer, 0, 0, 0)), lay2(D, D)],
        out_specs=row(D),
        out_shape=jax.ShapeDtypeStruct((T, D), F32),
        compiler_params=pltpu.CompilerParams(dimension_semantics=("parallel",), vmem_limit_bytes=VMEM_LIMIT),
        name="merge",
    )(x2d, *branches, P['n1'], P['w_gate'], P['b_gate'], P['w_branch'], P['w_out'])


def _mlp_kernel(x_ref, n2_ref, w1_ref, w2_ref, o_ref):
    x = x_ref[...]
    u = jnp.maximum(_dot(_rms(x, n2_ref[...]).astype(BF16), w1_ref[...]), 0.0)
    o_ref[...] = x + _dot((u * u).astype(BF16), w2_ref[...])


def _mlp(x2d, layer, P, tm):
    T, D = x2d.shape
    dff = P['w1'].shape[2]
    row = pl.BlockSpec((tm, D), lambda i: (i, 0))
    return pl.pallas_call(
        _mlp_kernel,
        grid=(T // tm,),
        in_specs=[row, pl.BlockSpec((None, 1, D), lambda i: (layer, 0, 0)),
                  pl.BlockSpec((None, D, dff), lambda i: (layer, 0, 0)),
                  pl.BlockSpec((None, dff, D), lambda i: (layer, 0, 0))],
        out_specs=row,
        out_shape=jax.ShapeDtypeStruct((T, D), F32),
        compiler_params=pltpu.CompilerParams(dimension_semantics=("parallel",), vmem_limit_bytes=VMEM_LIMIT),
        name="mlp",
    )(x2d, P['n2'], P['w1'], P['w2'])


def _prep_params(norm1_g, w_in, gla_wa2, gla_ba, gla_norm_g, hgrn_lb_param, hgrn_norm_g, pool_w, pool_scale,
                 nsa_q_norm_g, nsa_k_norm_g, w_branch, w_gate, b_gate, w_out, norm2_g, mlp_w1, mlp_w2):
    depth = w_in.shape[0]
    sizes = (QK, QK, VW, VW, GLA_RANK, QK, QK, VW, VW, 512, H_D * HD_D) + (KVH_D * HD_D,) * 6 + (H_D * 3,)
    offs = [0]
    for s in sizes:
        offs.append(offs[-1] + s)
    col = lambda i: w_in[:, :, offs[i]:offs[i + 1]]
    order = [0, 1, 2, 3, 5, 6, 7, 8, 9, 10, 11, 12, 13, 14, 15, 16, 4, 17]
    pad = jnp.zeros(w_in.shape[:2] + (LANES - GLA_RANK - H_D * 3,), w_in.dtype)
    w_in_r = jnp.concatenate([col(i) for i in order] + [pad], axis=-1).astype(BF16)
    wa2 = jnp.concatenate([gla_wa2, jnp.zeros((depth, LANES - GLA_RANK, QK), gla_wa2.dtype)], axis=1).astype(BF16)
    r3 = lambda a: a.reshape(depth, 1, a.shape[-1])
    lane = jnp.arange(LANES)
    seg = jnp.where((lane[:, None] // HD_D) == (lane[None, :] // HD_D), 1.0 / HD_D, 0.0).astype(BF16)
    w_kvt = jnp.transpose(w_in_r[:, :, OFF_KV:OFF_TAIL], (0, 2, 1))
    return dict(
        n1=r3(norm1_g), w_in=w_in_r, w_kvt=w_kvt, wa2=wa2, ba=r3(gla_ba), lbp=hgrn_lb_param,
        qg=r3(jnp.tile(nsa_q_norm_g, (1, H_D))), kg=jnp.tile(nsa_k_norm_g, (1, 1, KVH_D)), seg=seg,
        kgc=nsa_k_norm_g.reshape(depth, 3, HD_D, 1),
        gla_g=gla_norm_g, hgrn_g=hgrn_norm_g, pool_w=pool_w.astype(BF16), pool_sc=r3(pool_scale),
        w_gate=w_gate.astype(BF16), b_gate=r3(b_gate), w_branch=w_branch.astype(BF16), w_out=w_out.astype(BF16),
        n2=r3(norm2_g), w1=mlp_w1.astype(BF16), w2=mlp_w2.astype(BF16))


def _overlap(n_cmp_rows, n_lanes):
    per = SEL_BLOCK // CMP_STRIDE
    ci = jnp.arange(n_cmp_rows)[:, None]
    sj = jnp.arange(n_lanes)[None, :]
    return jnp.where((ci // per == sj) | ((ci + 1) // per == sj), 1.0, 0.0).astype(BF16)


def _prep_consts(L, past):
    TQ = 512
    KC = 512
    kpos = jnp.arange(L).reshape(L // KC, 1, KC)
    exp_p = jnp.where(kpos // SEL_BLOCK == jnp.arange(LANES)[None, :, None], 1.0, 0.0).astype(BF16)
    n16_d = past // CMP_STRIDE
    n_sel_d = (past + 1 + SEL_BLOCK - 1) // SEL_BLOCK
    n_lane_d = -(-n_sel_d // LANES) * LANES
    r = jnp.arange(L)[:, None]
    c0 = jnp.arange(L // CMP_STRIDE)[None, :] * CMP_STRIDE
    pool32_p = jnp.where((r >= c0) & (r < c0 + CMP_BLOCK), 1.0, 0.0).astype(BF16)
    return dict(TQ=TQ, KC=KC, ovl_p=_overlap(L // CMP_STRIDE, L // SEL_BLOCK).T, exp_p=exp_p, pool32_p=pool32_p,
                ovl_d=_overlap(n16_d, n_lane_d), n_sel_d=n_sel_d)


def kernel(x_prompt, x_sample, state_gla, state_hgrn, state_pool, cache_cmp_kv, cache_slc_kv, cache_win_kv,
           page_table, norm1_g, w_in, gla_wa2, gla_ba, gla_norm_g, hgrn_lb_param, hgrn_norm_g, pool_w,
           pool_scale, nsa_q_norm_g, nsa_k_norm_g, w_branch, w_gate, b_gate, w_out, norm2_g, mlp_w1, mlp_w2):
    B, L, D = x_prompt.shape
    Bd = x_sample.shape[0]
    depth = w_in.shape[0]
    n_pages = page_table.shape[1]
    past = n_pages * PAGE_SIZE
    assert x_sample.shape[1] == 1 and L % 512 == 0 and L >= WINDOW + 512 and past % SEL_BLOCK == 0
    assert cache_win_kv.shape[2] == WINDOW and past >= WINDOW

    P = _prep_params(norm1_g, w_in, gla_wa2, gla_ba, gla_norm_g, hgrn_lb_param, hgrn_norm_g, pool_w, pool_scale,
                     nsa_q_norm_g, nsa_k_norm_g, w_branch, w_gate, b_gate, w_out, norm2_g, mlp_w1, mlp_w2)
    C = _prep_consts(L, past)
    tm_p = 256
    cb = 128
    bb = 8 if Bd % 8 == 0 else Bd
    keep = min(WINDOW, L)

    pps = 32 if n_pages % 32 == 0 else n_pages
    cmp_fm = _feature_major(cache_cmp_kv).astype(BF16)
    slc_fm = _feature_major(cache_slc_kv).astype(BF16)
    win_fm = _feature_major(cache_win_kv).astype(BF16)
    s16_all = _page_sums(cmp_fm, page_table, pps)

    xp = x_prompt.reshape(B * L, D)
    xs = x_sample.reshape(Bd, D)
    outs = {k: [] for k in ('gla_p', 'gla_s', 'hgrn_p', 'hgrn_s', 'pool_p', 'pool_s', 'cmp_p', 'cmp_s',
                            'slc_p', 'slc_s', 'win_p', 'win_s')}
    zero_prev = jnp.zeros((B, POOL_HIST + 1, 512), F32)
    rows5 = lambda a, n: a.reshape(n, -1, 2, KVH_D, HD_D)
    for l in range(depth):
        gqkg, gvr, hqkg, hvr, u, nq, cmp_t, slc_c, win_c, tail = _in_proj(xp, l, P, tm_p, seq_len=L, kc=C['KC'])
        b3 = lambda a: a.reshape(B, L, a.shape[-1])
        o_gla, s_gla = _gla_prompt(b3(gqkg), b3(gvr), P['gla_g'][l:l + 1], cb)
        o_hgrn, s_hgrn = _gla_prompt(b3(hqkg), b3(hvr), P['hgrn_g'][l:l + 1], cb)
        o_pool = _pool(b3(u), zero_prev, l, P, 0)
        kcvc = _cmp_prompt(cmp_t, l, P, C)
        o_nsa = _nsa_prompt(b3(nq), b3(tail), kcvc, slc_c, win_c, C)
        flat = lambda a: a.reshape(B * L, a.shape[-1])
        xp = _merge(xp, (flat(o_gla), flat(o_hgrn), flat(o_pool), flat(o_nsa)), l, P, tm_p)
        xp = _mlp(xp, l, P, tm_p)
        outs['gla_p'].append(s_gla)
        outs['hgrn_p'].append(s_hgrn)
        outs['pool_p'].append(b3(u)[:, L - POOL_HIST:])
        unchunk = lambda a: jnp.transpose(a, (0, 2, 1, 3)).reshape(B, KV_ROW, -1)
        outs['cmp_p'].append(cmp_t)
        outs['slc_p'].append(unchunk(slc_c))
        outs['win_p'].append(unchunk(win_c[:, (L - keep) // LANES:]))

        gqkg, gvr, hqkg, hvr, u, nq, cmp_r, slc_r, win_r, tail = _in_proj(xs, l, P, Bd)
        o_gla, s_gla = _gla_decode(gqkg, gvr, P['gla_g'][l], state_gla, l, bb)
        o_hgrn, s_hgrn = _gla_decode(hqkg, hvr, P['hgrn_g'][l], state_hgrn, l, bb)
        prev = state_pool[l]
        prev16 = jnp.concatenate([jnp.zeros((Bd, 1, 512), F32), prev], axis=1)
        o_pool = _pool(u.reshape(Bd, 1, 512), prev16, l, P, past).reshape(Bd, 512)
        ocmp, idx = _nsa_dec_select(nq, s16_all, l, P, C, past)
        o_nsa = _nsa_dec_attend(nq, tail, ocmp, idx, slc_r, win_r, slc_fm, win_fm, page_table, l, C)
        xs = _merge(xs, (o_gla, o_hgrn, o_pool, o_nsa), l, P, Bd)
        xs = _mlp(xs, l, P, Bd)
        outs['gla_s'].append(s_gla)
        outs['hgrn_s'].append(s_hgrn)
        outs['pool_s'].append(jnp.concatenate([prev[:, 1:], u.reshape(Bd, 1, 512)], axis=1))
        outs['cmp_s'].append(rows5(cmp_r, Bd))
        outs['slc_s'].append(rows5(slc_r, Bd))
        outs['win_s'].append(rows5(win_r, Bd))

    st = lambda k: jnp.stack(outs[k])

    def rows_p(k):
        a = st(k)
        return jnp.transpose(a.reshape(depth, B, 2, KVH_D, HD_D, a.shape[-1]), (0, 1, 5, 2, 3, 4))

    return (xp.reshape(B, L, D), xs.reshape(Bd, 1, D),
            st('gla_p'), st('gla_s'), st('hgrn_p'), st('hgrn_s'), st('pool_p'), st('pool_s'),
            rows_p('cmp_p'), st('cmp_s'), rows_p('slc_p'), st('slc_s'), rows_p('win_p'), st('win_s'))
```

```python
import functools

import jax
import jax.numpy as jnp
from jax import lax
from jax.experimental import pallas as pl
from jax.experimental.pallas import tpu as pltpu

F32 = jnp.float32
BF16 = jnp.bfloat16
I32 = jnp.int32

N_BRANCH = 4
H_LIN = 4
DK_LIN = 64
DV_LIN = 128
GLA_RANK = 16
GLA_TAU = 16.0
POOL_WINDOWS = (2, 4, 8, 16)
POOL_HIST = 15
POOL_GROUP = 128
H_D = 8
HD_D = 64
KVH_D = 2
G_D = 4
CMP_STRIDE = 16
CMP_BLOCK = 32
SEL_BLOCK = 64
SEL_SHIFT = 6
TOPK = 16
WINDOW = 512
FORCE_BONUS = 1000.0
PAGE_SIZE = 128
PAGE_SHIFT = 7
EPS = 1e-6
NEG = -1e30
LOG2E = 1.4426950408889634

LANES = 128
VMEM_LIMIT = 56 * 1024 * 1024

QK = H_LIN * DK_LIN
VW = H_LIN * DV_LIN
OFF_GLA = 0
OFF_HGRN = OFF_GLA + 2 * QK + 2 * VW
OFF_POOL = OFF_HGRN + 2 * QK + 2 * VW
OFF_NQ = OFF_POOL + 512
OFF_KV = OFF_NQ + H_D * HD_D
OFF_TAIL = OFF_KV + 6 * KVH_D * HD_D
IN_PAD = OFF_TAIL + LANES
TAIL_GATE0 = GLA_RANK
KV_ROW = 2 * KVH_D * HD_D


def _dot(a, b):
    return jnp.dot(a, b, preferred_element_type=F32)


def _dot_nt(a, b):
    return lax.dot_general(a, b, (((1,), (1,)), ((), ())), preferred_element_type=F32)


def _rms(x, gain):
    return x * lax.rsqrt(jnp.mean(x * x, axis=-1, keepdims=True) + EPS) * gain


def _seg_mean(x2, seg):
    parts = []
    for c in range(x2.shape[1] // LANES):
        xc = x2[:, c * LANES:(c + 1) * LANES]
        h1 = xc.astype(BF16)
        r1 = xc - h1.astype(F32)
        h2 = r1.astype(BF16)
        h3 = (r1 - h2.astype(F32)).astype(BF16)
        parts.append(_dot(h1, seg) + _dot(h2, seg) + _dot(h3, seg))
    return parts[0] if len(parts) == 1 else jnp.concatenate(parts, axis=1)


def _seg_rms(x, gain, seg):
    return x * lax.rsqrt(_seg_mean(x * x, seg) + EPS) * gain


def _sigmoid(x):
    return 1.0 / (1.0 + jnp.exp(-x))


def _in_proj_kernel(layer, kv_t, x_ref, n1_ref, w_ref, wa2_ref, ba_ref, lbp_ref, qg_ref, kg_ref, seg_ref,
                    wkvt_ref, kgc_ref,
                    gqkg_ref, gvr_ref, hqkg_ref, hvr_ref, u_ref, nq_ref, cmp_ref, slc_ref, win_ref, tail_ref):
    x = x_ref[...]
    hb = _rms(x, n1_ref[...]).astype(BF16)
    seg = seg_ref[...]

    def proj(c0, c1):
        return _dot(hb, w_ref[:, c0:c1])

    tail = proj(OFF_TAIL, IN_PAD)
    tail_ref[...] = _sigmoid(tail)

    a_pre = _dot(tail.astype(BF16), wa2_ref[...]) + ba_ref[...]
    log_a = (jnp.minimum(a_pre, 0.0) - jnp.log(1.0 + jnp.exp(-jnp.abs(a_pre)))) / GLA_TAU
    z = proj(OFF_GLA, OFF_GLA + 2 * QK)
    gqkg_ref[:, 0:QK] = z[:, 0:QK] * (DK_LIN ** -0.5)
    gqkg_ref[:, QK:2 * QK] = z[:, QK:2 * QK]
    gqkg_ref[:, 2 * QK:3 * QK] = log_a
    gvr_ref[...] = proj(OFF_GLA + 2 * QK, OFF_HGRN)

    lbp = lbp_ref[...]
    pe = jnp.exp(lbp - jnp.max(lbp, axis=0, keepdims=True))
    soft = pe / jnp.sum(pe, axis=0, keepdims=True)
    lb = jnp.zeros((1, QK), F32)
    for i in range(1, layer + 1):
        lb = lb + soft[i:i + 1, :]
    z = proj(OFF_HGRN, OFF_HGRN + 2 * QK)
    hq = z[:, 0:QK]
    fgate = lb + (1.0 - lb) * _sigmoid(z[:, QK:2 * QK])
    hqkg_ref[:, 0:QK] = hq * _sigmoid(hq) * (DK_LIN ** -0.5)
    hqkg_ref[:, QK:2 * QK] = 1.0 - fgate
    hqkg_ref[:, 2 * QK:3 * QK] = jnp.log(fgate)
    hvr_ref[...] = proj(OFF_HGRN + 2 * QK, OFF_POOL)

    u_ref[...] = proj(OFF_POOL, OFF_NQ)
    nq_ref[...] = _seg_rms(proj(OFF_NQ, OFF_KV), qg_ref[...], seg)

    if kv_t:
        zt = _dot_nt(wkvt_ref[...], hb)
        cmp_ref[0] = zt[0:KV_ROW]

        def normed_rows(base, gi):
            ks = [_norm_feature_major(zt[base + kvh * HD_D:base + (kvh + 1) * HD_D], kgc_ref[gi])
                  for kvh in range(KVH_D)]
            return jnp.concatenate(ks + [zt[base + LANES:base + KV_ROW]], axis=0)

        slc_ref[0, 0] = normed_rows(KV_ROW, 1)
        win = normed_rows(2 * KV_ROW, 2)
        for c in range(win_ref.shape[1]):
            win_ref[0, c] = win[:, c * LANES:(c + 1) * LANES]
    else:
        zkv = proj(OFF_KV, OFF_TAIL)
        cmp_ref[...] = zkv[:, 0:KV_ROW]
        slc_ref[:, 0:LANES] = _seg_rms(zkv[:, 256:384], kg_ref[1:2, :], seg)
        slc_ref[:, LANES:KV_ROW] = zkv[:, 384:512]
        win_ref[:, 0:LANES] = _seg_rms(zkv[:, 512:640], kg_ref[2:3, :], seg)
        win_ref[:, LANES:KV_ROW] = zkv[:, 640:768]


def _in_proj(x2d, layer, P, tm, seq_len=None, kc=None):
    T, D = x2d.shape
    depth = P['lbp'].shape[0]
    kv_t = seq_len is not None
    row = lambda w: pl.BlockSpec((tm, w), lambda i: (i, 0))
    lay2 = lambda a, b: pl.BlockSpec((None, a, b), lambda i: (layer, 0, 0))
    widths = (3 * QK, 2 * VW, 3 * QK, 2 * VW, 512, H_D * HD_D)
    out_specs = [row(w) for w in widths]
    out_shape = [jax.ShapeDtypeStruct((T, w), F32) for w in widths]
    if kv_t:
        nt = seq_len // tm
        per = kc // tm
        nb = T // seq_len
        out_specs += [pl.BlockSpec((1, KV_ROW, tm), lambda i: (i // nt, 0, i % nt)),
                      pl.BlockSpec((1, 1, KV_ROW, tm), lambda i: (i // nt, (i % nt) // per, 0, (i % nt) % per)),
                      pl.BlockSpec((1, tm // LANES, KV_ROW, LANES), lambda i: (i // nt, i % nt, 0, 0))]
        out_shape += [jax.ShapeDtypeStruct((nb, KV_ROW, seq_len), F32),
                      jax.ShapeDtypeStruct((nb, seq_len // kc, KV_ROW, kc), F32),
                      jax.ShapeDtypeStruct((nb, seq_len // LANES, KV_ROW, LANES), F32)]
    else:
        out_specs += [row(KV_ROW)] * 3
        out_shape += [jax.ShapeDtypeStruct((T, KV_ROW), F32)] * 3
    out_specs.append(row(LANES))
    out_shape.append(jax.ShapeDtypeStruct((T, LANES), F32))
    return pl.pallas_call(
        functools.partial(_in_proj_kernel, layer, kv_t),
        grid=(T // tm,),
        in_specs=[row(D), lay2(1, D), lay2(D, IN_PAD), lay2(LANES, QK), lay2(1, QK),
                  pl.BlockSpec((depth, QK), lambda i: (0, 0)), lay2(1, H_D * HD_D), lay2(3, LANES),
                  pl.BlockSpec((LANES, LANES), lambda i: (0, 0)), lay2(3 * KV_ROW, D),
                  pl.BlockSpec((None, 3, HD_D, 1), lambda i: (layer, 0, 0, 0))],
        out_specs=out_specs,
        out_shape=out_shape,
        compiler_params=pltpu.CompilerParams(dimension_semantics=("parallel",), vmem_limit_bytes=VMEM_LIMIT),
        name="in_proj",
    )(x2d, P['n1'], P['w_in'], P['wa2'], P['ba'], P['lbp'], P['qg'], P['kg'], P['seg'], P['w_kvt'], P['kgc'])


def _gla_kernel(levels, qkg_ref, vr_ref, gain_ref, o_ref, s_ref):
    CB = qkg_ref.shape[1]

    @pl.when(pl.program_id(1) == 0)
    def _():
        s_ref[...] = jnp.zeros_like(s_ref)

    q = qkg_ref[0, :, 0:QK]
    k = qkg_ref[0, :, QK:2 * QK]
    g = qkg_ref[0, :, 2 * QK:3 * QK]
    row = lax.broadcasted_iota(I32, (CB, 1), 0)
    r2 = lax.broadcasted_iota(I32, (CB, CB), 0)
    c2 = lax.broadcasted_iota(I32, (CB, CB), 1)
    heads = [slice(h * DK_LIN, (h + 1) * DK_LIN) for h in range(H_LIN)]

    qb = q.astype(BF16)
    kb = k.astype(BF16)
    att = [jnp.where(r2 == c2, _dot_nt(qb[:, hs], kb[:, hs]), 0.0) for hs in heads]
    c = g
    tot = g
    for lvl in range(levels):
        half = 1 << lvl
        upper = ((row >> lvl) & 1) == 1
        qt = jnp.where(upper, q * jnp.exp(c), 0.0).astype(BF16)
        kt = jnp.where(upper, 0.0, k * jnp.exp(tot - c)).astype(BF16)
        same = (r2 >> (lvl + 1)) == (c2 >> (lvl + 1))
        for h, hs in enumerate(heads):
            att[h] = att[h] + jnp.where(same, _dot_nt(qt[:, hs], kt[:, hs]), 0.0)
        t_lo = pltpu.roll(tot, half, 0)
        t_hi = pltpu.roll(tot, CB - half, 0)
        c = c + jnp.where(upper, t_lo, 0.0)
        tot = tot + jnp.where(upper, t_lo, t_hi)

    qd = (q * jnp.exp(c)).astype(BF16)
    kdt = jnp.transpose(k * jnp.exp(tot - c)).astype(BF16)
    eye = (lax.broadcasted_iota(I32, (DK_LIN, DK_LIN), 0) == lax.broadcasted_iota(I32, (DK_LIN, DK_LIN), 1))
    gain = gain_ref[...]
    for h, hs in enumerate(heads):
        vs = slice(h * DV_LIN, (h + 1) * DV_LIN)
        v = vr_ref[0, :, vs].astype(BF16)
        s_old = s_ref[0, h]
        o = _dot(att[h].astype(BF16), v) + _dot(qd[:, hs], s_old.astype(BF16))
        t_row = jnp.broadcast_to(tot[0:1, hs], (DK_LIN, DK_LIN))
        t_col = jnp.sum(jnp.where(eye, t_row, 0.0), axis=1, keepdims=True)
        s_ref[0, h] = s_old * jnp.exp(t_col) + _dot(kdt[hs, :], v)
        r = vr_ref[0, :, VW + h * DV_LIN:VW + (h + 1) * DV_LIN]
        o_ref[0, :, vs] = _rms(o, gain) * (r * _sigmoid(r))


def _gla_prompt(qkg, vr, gain, cb):
    B, L, _ = qkg.shape
    levels = cb.bit_length() - 1
    return pl.pallas_call(
        functools.partial(_gla_kernel, levels),
        grid=(B, L // cb),
        in_specs=[pl.BlockSpec((1, cb, 3 * QK), lambda b, j: (b, j, 0)),
                  pl.BlockSpec((1, cb, 2 * VW), lambda b, j: (b, j, 0)),
                  pl.BlockSpec((1, DV_LIN), lambda b, j: (0, 0))],
        out_specs=[pl.BlockSpec((1, cb, VW), lambda b, j: (b, j, 0)),
                   pl.BlockSpec((1, H_LIN, DK_LIN, DV_LIN), lambda b, j: (b, 0, 0, 0))],
        out_shape=[jax.ShapeDtypeStruct((B, L, VW), F32),
                   jax.ShapeDtypeStruct((B, H_LIN, DK_LIN, DV_LIN), F32)],
        compiler_params=pltpu.CompilerParams(dimension_semantics=("parallel", "arbitrary"),
                                             vmem_limit_bytes=VMEM_LIMIT),
        name="gla_prompt",
    )(qkg, vr, gain)


def _gla_dec_kernel(q_ref, k_ref, g_ref, v_ref, r_ref, gain_ref, s0_ref, o_ref, s_ref):
    s_new = s0_ref[...] * jnp.exp(g_ref[...]) + k_ref[...] * v_ref[...]
    s_ref[...] = s_new
    o = jnp.sum(q_ref[...] * s_new, axis=2, keepdims=True)
    r = r_ref[...]
    o_ref[...] = _rms(o, gain_ref[...]) * (r * _sigmoid(r))


def _gla_decode(qkg, vr, gain, state, layer, bb):
    Bd = qkg.shape[0]
    col = lambda a: a.reshape(Bd, H_LIN, DK_LIN, 1)
    rowv = lambda a: a.reshape(Bd, H_LIN, 1, DV_LIN)
    q, k, g = col(qkg[:, 0:QK]), col(qkg[:, QK:2 * QK]), col(qkg[:, 2 * QK:])
    v, r = rowv(vr[:, 0:VW]), rowv(vr[:, VW:])
    cspec = pl.BlockSpec((bb, H_LIN, DK_LIN, 1), lambda i: (i, 0, 0, 0))
    rspec = pl.BlockSpec((bb, H_LIN, 1, DV_LIN), lambda i: (i, 0, 0, 0))
    o, s_new = pl.pallas_call(
        _gla_dec_kernel,
        grid=(Bd // bb,),
        in_specs=[cspec, cspec, cspec, rspec, rspec,
                  pl.BlockSpec((1, 1, 1, DV_LIN), lambda i: (0, 0, 0, 0)),
                  pl.BlockSpec((None, bb, H_LIN, DK_LIN, DV_LIN), lambda i: (layer, i, 0, 0, 0))],
        out_specs=[rspec, pl.BlockSpec((bb, H_LIN, DK_LIN, DV_LIN), lambda i: (i, 0, 0, 0))],
        out_shape=[jax.ShapeDtypeStruct((Bd, H_LIN, 1, DV_LIN), F32),
                   jax.ShapeDtypeStruct((Bd, H_LIN, DK_LIN, DV_LIN), F32)],
        compiler_params=pltpu.CompilerParams(dimension_semantics=("parallel",), vmem_limit_bytes=VMEM_LIMIT),
        name="gla_decode",
    )(q, k, g, v, r, gain.reshape(1, 1, 1, DV_LIN), state)
    return o.reshape(Bd, VW), s_new


def _pool_kernel(pos0, rt, u_ref, prev_ref, w_ref, sc_ref, o_ref, ext_ref):
    L = u_ref.shape[1]
    hdr = POOL_HIST + 1
    ext_ref[0:hdr, :] = prev_ref[0]
    ext_ref[hdr:hdr + L, :] = u_ref[0]
    for r0 in range(0, L, rt):
        n_avail = (pos0 + r0 + 1 + lax.broadcasted_iota(I32, (rt, 1), 0)).astype(F32)
        for gi, w in enumerate(POOL_WINDOWS):
            cs = slice(gi * POOL_GROUP, (gi + 1) * POOL_GROUP)
            tok = ext_ref[hdr + r0:hdr + r0 + rt, cs]
            acc = tok
            for j in range(1, w):
                acc = acc + ext_ref[hdr + r0 - j:hdr + r0 - j + rt, cs]
            d = acc / jnp.minimum(float(w), n_avail) - tok
            o_ref[0, r0:r0 + rt, cs] = _dot(d.astype(BF16), w_ref[gi]) * sc_ref[:, cs]


def _pool(u, prev16, layer, P, pos0):
    B, L, C = u.shape
    rt = min(L, 256)
    return pl.pallas_call(
        functools.partial(_pool_kernel, pos0, rt),
        grid=(B,),
        in_specs=[pl.BlockSpec((1, L, C), lambda b: (b, 0, 0)),
                  pl.BlockSpec((1, POOL_HIST + 1, C), lambda b: (b, 0, 0)),
                  pl.BlockSpec((None, len(POOL_WINDOWS), POOL_GROUP, POOL_GROUP), lambda b: (layer, 0, 0, 0)),
                  pl.BlockSpec((None, 1, C), lambda b: (layer, 0, 0))],
        out_specs=pl.BlockSpec((1, L, C), lambda b: (b, 0, 0)),
        out_shape=jax.ShapeDtypeStruct((B, L, C), F32),
        scratch_shapes=[pltpu.VMEM((POOL_HIST + 1 + L, C), F32)],
        compiler_params=pltpu.CompilerParams(dimension_semantics=("parallel",), vmem_limit_bytes=VMEM_LIMIT),
        name="pool",
    )(u, prev16, P['pool_w'], P['pool_sc'])


def _pool_lanes(x, pm):
    hi = x.astype(BF16)
    lo = (x - hi.astype(F32)).astype(BF16)
    return _dot(hi, pm) + _dot(lo, pm)


def _norm_feature_major(x, gain_col):
    return x * lax.rsqrt(jnp.mean(x * x, axis=0, keepdims=True) + EPS) * gain_col


def _cmp_store(cmp, gain_col, o_ref):
    for kvh in range(KVH_D):
        rows = slice(kvh * HD_D, (kvh + 1) * HD_D)
        o_ref[0, rows, :] = _norm_feature_major(cmp[rows], gain_col)
    o_ref[0, LANES:KV_ROW, :] = cmp[LANES:KV_ROW]


def _cmp_prompt_kernel(rows_ref, kgc_ref, pm_ref, o_ref):
    _cmp_store(_pool_lanes(rows_ref[0], pm_ref[...]) / float(CMP_BLOCK), kgc_ref[0], o_ref)


def _cmp_prompt(rows_t, layer, P, C):
    B, _, L = rows_t.shape
    n16 = L // CMP_STRIDE
    return pl.pallas_call(
        _cmp_prompt_kernel,
        grid=(B,),
        in_specs=[pl.BlockSpec((1, KV_ROW, L), lambda b: (b, 0, 0)),
                  pl.BlockSpec((None, 3, HD_D, 1), lambda b: (layer, 0, 0, 0)),
                  pl.BlockSpec((L, n16), lambda b: (0, 0))],
        out_specs=pl.BlockSpec((1, KV_ROW, n16), lambda b: (b, 0, 0)),
        out_shape=jax.ShapeDtypeStruct((B, KV_ROW, n16), F32),
        compiler_params=pltpu.CompilerParams(dimension_semantics=("parallel",), vmem_limit_bytes=VMEM_LIMIT),
        name="cmp_prompt",
    )(rows_t, P['kgc'], C['pool32_p'])


def _feature_major(cache):
    nd = cache.ndim
    return jnp.transpose(cache, tuple(range(nd - 4)) + (nd - 3, nd - 2, nd - 1, nd - 4))


def _page_sum_kernel(n_in, pt_ref, *refs):
    pm_ref, o_ref = refs[n_in], refs[n_in + 1]
    x = jnp.concatenate([refs[i][0] for i in range(n_in)], axis=1)
    o_ref[0, 0] = _dot(x, pm_ref[...])


def _page_sums(cache, page_table, pages_per_step):
    depth, n_phys = cache.shape[0], cache.shape[1]
    Bd, n_pages = page_table.shape
    flat = cache.reshape(depth * n_phys, KV_ROW, PAGE_SIZE)
    per = PAGE_SIZE // CMP_STRIDE
    n_in = pages_per_step
    pos = jnp.arange(n_in * PAGE_SIZE)
    pm = jnp.where(pos[:, None] // CMP_STRIDE == jnp.arange(n_in * per)[None, :], 1.0, 0.0).astype(BF16)

    def page_spec(i):
        return pl.BlockSpec((1, KV_ROW, PAGE_SIZE),
                            lambda l, b, p, pt: (l * n_phys + pt[b, p * n_in + i], 0, 0))

    return pl.pallas_call(
        functools.partial(_page_sum_kernel, n_in),
        grid_spec=pltpu.PrefetchScalarGridSpec(
            num_scalar_prefetch=1,
            grid=(depth, Bd, n_pages // n_in),
            in_specs=[page_spec(i) for i in range(n_in)]
                     + [pl.BlockSpec((n_in * PAGE_SIZE, n_in * per), lambda l, b, p, pt: (0, 0))],
            out_specs=pl.BlockSpec((1, 1, KV_ROW, n_in * per), lambda l, b, p, pt: (l, b, 0, p))),
        out_shape=jax.ShapeDtypeStruct((depth, Bd, KV_ROW, n_pages * per), F32),
        compiler_params=pltpu.CompilerParams(dimension_semantics=("parallel", "parallel", "arbitrary"),
                                             vmem_limit_bytes=VMEM_LIMIT),
        name="page_sums",
    )(page_table, *([flat] * n_in), pm)


def _rank_count(score, idx, n_blocks, axis):
    cnt = jnp.zeros(score.shape, F32)
    for m in range(n_blocks):
        other = score[m:m + 1, :] if axis == 0 else score[:, m:m + 1]
        beats = jnp.where(other > score, 1.0, jnp.where(other == score, jnp.where(idx > m, 1.0, 0.0), 0.0))
        cnt = cnt + beats
    return cnt


def _softmax2_rows(s2, mask):
    sm = jnp.where(mask, s2, NEG)
    p = jnp.where(mask, jnp.exp2(sm - jnp.max(sm, axis=-1, keepdims=True)), 0.0)
    den = jnp.sum(p, axis=-1, keepdims=True)
    return p / jnp.where(den > 0.0, den, 1.0)


def _nsa_prompt_kernel(TQ, KC, n_sel, q_ref, tail_ref, kcvc_ref, slc_ref, win_ref, ovl_ref, exp_ref, o_ref):
    t0 = pl.program_id(1) * TQ
    R = G_D * TQ
    t = t0 + lax.broadcasted_iota(I32, (TQ, 1), 0)
    t4 = t0 + (lax.broadcasted_iota(I32, (R, 1), 0) & (TQ - 1))
    rep = lambda a: jnp.concatenate([a] * G_D, axis=0)
    q = q_ref[0]
    gates = tail_ref[0]
    n_cmp = kcvc_ref.shape[2]
    lane_c = lax.broadcasted_iota(I32, (1, n_cmp), 1)
    cmask4 = (lane_c * CMP_STRIDE + (CMP_BLOCK - 1)) <= t4
    blk = lax.broadcasted_iota(I32, (n_sel, 1), 0)
    cur = (t0 + lax.broadcasted_iota(I32, (1, TQ), 1)) >> SEL_SHIFT
    j_last = t0 // KC
    k_sel = min(TOPK, n_sel)
    add_rows = lambda s, bias: (s.reshape(G_D, TQ, s.shape[1]) + bias[None]).reshape(R, s.shape[1])
    lane_k = lax.broadcasted_iota(I32, (1, KC), 1)
    WK = WINDOW + TQ
    ws = jnp.maximum(t0 - WINDOW, 0)
    wc = ws // LANES
    dist = t - (ws + lax.broadcasted_iota(I32, (1, WK), 1))
    bias_w = jnp.where((dist >= 0) & (dist <= WINDOW), 0.0, NEG)

    for kvh in range(KVH_D):
        ksl = slice(kvh * HD_D, (kvh + 1) * HD_D)
        vsl = slice(LANES + kvh * HD_D, LANES + (kvh + 1) * HD_D)
        qs = (jnp.concatenate([q[:, (kvh * G_D + g) * HD_D:(kvh * G_D + g + 1) * HD_D] for g in range(G_D)],
                              axis=0) * (HD_D ** -0.5 * LOG2E)).astype(BF16)
        pcf = _softmax2_rows(_dot(qs, kcvc_ref[0, ksl, :].astype(BF16)), cmask4)
        o_cmp = _dot_nt(pcf.astype(BF16), kcvc_ref[0, vsl, :].astype(BF16))
        pcs = pcf[0:TQ]
        for g in range(1, G_D):
            pcs = pcs + pcf[g * TQ:(g + 1) * TQ]
        imp = _dot_nt(ovl_ref[...], pcs.astype(BF16))
        forced = (blk == 0) | (blk == cur) | (blk == cur - 1)
        score = jnp.where(blk <= cur, imp + jnp.where(forced, FORCE_BONUS, 0.0), -jnp.inf)
        cnt = _rank_count(score, blk, n_sel, 0)
        selb = jnp.where(cnt < k_sel, 0.0, NEG)
        selb = jnp.transpose(jnp.concatenate([selb, jnp.zeros((LANES - n_sel, TQ), F32)], axis=0))
        q_aug = jnp.concatenate([rep(selb.astype(BF16)), qs], axis=1)

        def chunk(j, carry, q_aug=q_aug, ksl=ksl, vsl=vsl):
            m_i, l_i, acc = carry
            k_aug = jnp.concatenate([exp_ref[j], slc_ref[0, j, ksl, :].astype(BF16)], axis=0)
            causal = jnp.where(j * KC + lane_k <= t, 0.0, NEG)
            s = add_rows(_dot(q_aug, k_aug), causal)
            m_new = jnp.maximum(m_i, jnp.max(s, axis=-1, keepdims=True))
            alpha = jnp.exp2(m_i - m_new)
            p = jnp.exp2(s - m_new)
            l_new = alpha * l_i + jnp.sum(p, axis=-1, keepdims=True)
            acc = alpha * acc + _dot_nt(p.astype(BF16), slc_ref[0, j, vsl, :].astype(BF16))
            return m_new, l_new, acc

        init = (jnp.full((R, 1), NEG, F32), jnp.zeros((R, 1), F32), jnp.zeros((R, HD_D), F32))
        _, l_f, acc_f = lax.fori_loop(0, j_last + 1, chunk, init)
        o_slc = acc_f / l_f

        win_rows = lambda sl: jnp.concatenate(
            [win_ref[0, wc + i, sl, :].astype(BF16) for i in range(WK // LANES)], axis=1)
        sw = add_rows(_dot(qs, win_rows(ksl)), bias_w)
        pw = jnp.exp2(sw - jnp.max(sw, axis=-1, keepdims=True))
        o_win = _dot_nt(pw.astype(BF16), win_rows(vsl)) / jnp.sum(pw, axis=-1, keepdims=True)

        for g in range(G_D):
            hd = kvh * G_D + g
            rows = slice(g * TQ, (g + 1) * TQ)
            gc = TAIL_GATE0 + hd * 3
            o_ref[0, :, hd * HD_D:(hd + 1) * HD_D] = (gates[:, gc:gc + 1] * o_cmp[rows]
                                                     + gates[:, gc + 1:gc + 2] * o_slc[rows]
                                                     + gates[:, gc + 2:gc + 3] * o_win[rows])


def _nsa_prompt(nq, tail, kcvc, slc, win, C):
    B, L, _ = nq.shape
    TQ, KC = C['TQ'], C['KC']
    n_sel = L // SEL_BLOCK
    n16 = L // CMP_STRIDE
    qspec = lambda w: pl.BlockSpec((1, TQ, w), lambda b, i: (b, i, 0))
    full = lambda n: pl.BlockSpec((1, KV_ROW, n), lambda b, i: (b, 0, 0))
    return pl.pallas_call(
        functools.partial(_nsa_prompt_kernel, TQ, KC, n_sel),
        grid=(B, L // TQ),
        in_specs=[qspec(H_D * HD_D), qspec(LANES), full(n16),
                  pl.BlockSpec((1, L // KC, KV_ROW, KC), lambda b, i: (b, 0, 0, 0)),
                  pl.BlockSpec((1, L // LANES, KV_ROW, LANES), lambda b, i: (b, 0, 0, 0)),
                  pl.BlockSpec((n_sel, n16), lambda b, i: (0, 0)),
                  pl.BlockSpec((L // KC, LANES, KC), lambda b, i: (0, 0, 0))],
        out_specs=qspec(H_D * HD_D),
        out_shape=jax.ShapeDtypeStruct((B, L, H_D * HD_D), F32),
        compiler_params=pltpu.CompilerParams(dimension_semantics=("parallel", "arbitrary"),
                                             vmem_limit_bytes=VMEM_LIMIT),
        name="nsa_prompt",
    )(nq, tail, kcvc, slc, win, C['ovl_p'], C['exp_p'])


def _stack_heads(qrow, kvh):
    return jnp.concatenate([qrow[:, (kvh * G_D + g) * HD_D:(kvh * G_D + g + 1) * HD_D] for g in range(G_D)],
                           axis=0) * (HD_D ** -0.5 * LOG2E)


def _nsa_dec_select_kernel(t_now, n_sel, q_ref, s16_ref, kgc_ref, ovl_ref, ocmp_ref, idx_ref):
    s16 = s16_ref[0, 0]
    n_cmp = s16.shape[1]
    nxt = jnp.where(lax.broadcasted_iota(I32, (1, n_cmp), 1) == n_cmp - 1, 0.0, pltpu.roll(s16, n_cmp - 1, 1))
    cmp = (s16 + nxt) / float(CMP_BLOCK)
    lane_c = lax.broadcasted_iota(I32, (G_D, n_cmp), 1)
    cmask = (lane_c * CMP_STRIDE + (CMP_BLOCK - 1)) <= t_now
    n_lane = ovl_ref.shape[1]
    lane = lax.broadcasted_iota(I32, (1, n_lane), 1)
    cur = t_now // SEL_BLOCK
    k_sel = min(TOPK, n_sel)
    q = q_ref[0]
    for kvh in range(KVH_D):
        qs = _stack_heads(q, kvh).astype(BF16)
        kc = _norm_feature_major(cmp[kvh * HD_D:(kvh + 1) * HD_D], kgc_ref[0])
        pc = _softmax2_rows(_dot(qs, kc.astype(BF16)), cmask).astype(BF16)
        o_cmp = _dot_nt(pc, cmp[LANES + kvh * HD_D:LANES + (kvh + 1) * HD_D].astype(BF16))
        for g in range(G_D):
            hd = kvh * G_D + g
            ocmp_ref[0, :, hd * HD_D:(hd + 1) * HD_D] = o_cmp[g:g + 1, :]
        imp = jnp.sum(_dot(pc, ovl_ref[...]), axis=0, keepdims=True)
        forced = (lane == 0) | (lane == cur) | (lane == cur - 1)
        score = jnp.where(lane <= cur, imp + jnp.where(forced, FORCE_BONUS, 0.0), -jnp.inf)
        cnt = _rank_count(score, lane, n_sel, 1)
        lane_f = lane.astype(F32)
        row = jnp.zeros((1, LANES), F32)
        out_lane = lax.broadcasted_iota(I32, (1, LANES), 1)
        for r in range(k_sel):
            pick = jnp.sum(jnp.where(cnt == float(r), lane_f, 0.0), axis=1, keepdims=True)
            row = jnp.where(out_lane == r, pick, row)
        idx_ref[0, kvh:kvh + 1, :] = row.astype(I32)


def _nsa_dec_select(nq, s16_all, layer, P, C, t_now):
    Bd = nq.shape[0]
    n16 = s16_all.shape[3]
    n_sel = C['n_sel_d']
    n_lane = C['ovl_d'].shape[1]
    return pl.pallas_call(
        functools.partial(_nsa_dec_select_kernel, t_now, n_sel),
        grid=(Bd,),
        in_specs=[pl.BlockSpec((1, 1, H_D * HD_D), lambda b: (b, 0, 0)),
                  pl.BlockSpec((1, 1, KV_ROW, n16), lambda b: (layer, b, 0, 0)),
                  pl.BlockSpec((None, 3, HD_D, 1), lambda b: (layer, 0, 0, 0)),
                  pl.BlockSpec((n16, n_lane), lambda b: (0, 0))],
        out_specs=[pl.BlockSpec((1, 1, H_D * HD_D), lambda b: (b, 0, 0)),
                   pl.BlockSpec((1, KVH_D, LANES), lambda b: (b, 0, 0))],
        out_shape=[jax.ShapeDtypeStruct((Bd, 1, H_D * HD_D), F32),
                   jax.ShapeDtypeStruct((Bd, KVH_D, LANES), I32)],
        compiler_params=pltpu.CompilerParams(dimension_semantics=("parallel",), vmem_limit_bytes=VMEM_LIMIT),
        name="nsa_dec_select",
    )(nq.reshape(Bd, 1, H_D * HD_D), s16_all, P['kgc'], C['ovl_d'])


def _attend_rows(qs, kt, vt, valid, k_new, v_new):
    s = _dot(qs.astype(BF16), kt.astype(BF16))
    if valid is not None:
        s = jnp.where(valid > 0, s, NEG)
    s_new = jnp.sum(qs * k_new, axis=1, keepdims=True)
    m = jnp.maximum(jnp.max(s, axis=-1, keepdims=True), s_new)
    p = jnp.exp2(s - m)
    if valid is not None:
        p = jnp.where(valid > 0, p, 0.0)
    p_new = jnp.exp2(s_new - m)
    den = jnp.sum(p, axis=-1, keepdims=True) + p_new
    return (_dot_nt(p.astype(BF16), vt.astype(BF16)) + p_new * v_new) / den


def _nsa_dec_attend_kernel(n_past_blocks, k_sel, pt_ref, ix_ref, q_ref, tail_ref, ocmp_ref, slcn_ref, winn_ref,
                           winc_ref, *refs):
    blk_refs = refs[:2 * KVH_D * k_sel]
    o_ref = refs[2 * KVH_D * k_sel]
    b = pl.program_id(0)
    q = q_ref[0]
    gates = tail_ref[0]
    ocmp = ocmp_ref[0]
    slc_new = slcn_ref[0]
    win_new = winn_ref[0]
    per_page = PAGE_SIZE // SEL_BLOCK
    lane_pos = lax.broadcasted_iota(I32, (G_D, k_sel * PAGE_SIZE), 1)
    lane_slot = lane_pos >> PAGE_SHIFT
    lane_sub = (lane_pos >> SEL_SHIFT) & (per_page - 1)
    for kvh in range(KVH_D):
        ksl = slice(kvh * HD_D, (kvh + 1) * HD_D)
        vsl = slice(LANES + kvh * HD_D, LANES + (kvh + 1) * HD_D)
        qs = _stack_heads(q, kvh)
        slab = lambda kv: jnp.concatenate(
            [blk_refs[(kvh * k_sel + r) * 2 + kv][0] for r in range(k_sel)], axis=1)
        kt, vt = slab(0), slab(1)
        valid = jnp.zeros((G_D, k_sel * PAGE_SIZE), I32)
        for r in range(k_sel):
            blk = ix_ref[b, kvh * LANES + r]
            in_past = (blk < n_past_blocks).astype(I32)
            valid = jnp.where((lane_slot == r) & (lane_sub == blk % per_page), in_past, valid)
        o_slc = _attend_rows(qs, kt, vt, valid, slc_new[:, ksl], slc_new[:, vsl])
        o_win = _attend_rows(qs, winc_ref[0, ksl, :], winc_ref[0, vsl, :], None, win_new[:, ksl], win_new[:, vsl])
        for g in range(G_D):
            hd = kvh * G_D + g
            gc = TAIL_GATE0 + hd * 3
            hsl = slice(hd * HD_D, (hd + 1) * HD_D)
            o_ref[0, :, hsl] = (gates[:, gc:gc + 1] * ocmp[:, hsl] + gates[:, gc + 1:gc + 2] * o_slc[g:g + 1, :]
                                + gates[:, gc + 2:gc + 3] * o_win[g:g + 1, :])


def _nsa_dec_attend(nq, tail, ocmp, idx, slc_new, win_new, cache_slc, cache_win, page_table, layer, C):
    Bd = nq.shape[0]
    depth, n_phys = cache_slc.shape[0], cache_slc.shape[1]
    n_past_blocks = page_table.shape[1] * (PAGE_SIZE // SEL_BLOCK)
    k_sel = min(TOPK, C['n_sel_d'])
    per_page = PAGE_SIZE // SEL_BLOCK
    slc_flat = cache_slc.reshape(depth * n_phys * 2 * KVH_D, HD_D, PAGE_SIZE)
    n_win = cache_win.shape[-1]
    win_flat = cache_win.reshape(depth * Bd, KV_ROW, n_win)

    def blk_spec(kvh, r, kv):
        def imap(b, pt, ix):
            blk = jnp.minimum(ix[b, kvh * LANES + r], n_past_blocks - 1)
            return (((layer * n_phys + pt[b, blk // per_page]) * 2 + kv) * KVH_D + kvh, 0, 0)
        return pl.BlockSpec((1, HD_D, PAGE_SIZE), imap)

    one = lambda w: pl.BlockSpec((1, 1, w), lambda b, pt, ix: (b, 0, 0))
    r3 = lambda a: a.reshape(Bd, 1, a.shape[-1])
    o = pl.pallas_call(
        functools.partial(_nsa_dec_attend_kernel, n_past_blocks, k_sel),
        grid_spec=pltpu.PrefetchScalarGridSpec(
            num_scalar_prefetch=2,
            grid=(Bd,),
            in_specs=[one(H_D * HD_D), one(LANES), one(H_D * HD_D), one(KV_ROW), one(KV_ROW),
                      pl.BlockSpec((1, KV_ROW, n_win), lambda b, pt, ix: (layer * Bd + b, 0, 0))]
                     + [blk_spec(kvh, r, kv) for kvh in range(KVH_D) for r in range(k_sel) for kv in range(2)],
            out_specs=one(H_D * HD_D)),
        out_shape=jax.ShapeDtypeStruct((Bd, 1, H_D * HD_D), F32),
        compiler_params=pltpu.CompilerParams(dimension_semantics=("arbitrary",), vmem_limit_bytes=VMEM_LIMIT),
        name="nsa_dec_attend",
    )(page_table, idx.reshape(Bd, KVH_D * LANES), r3(nq), r3(tail), ocmp, r3(slc_new), r3(win_new), win_flat,
      *([slc_flat] * (2 * KVH_D * k_sel)))
    return o.reshape(Bd, H_D * HD_D)


def _merge_kernel(x_ref, b0_ref, b1_ref, b2_ref, b3_ref, n1_ref, wg_ref, bg_ref, wb_ref, wo_ref, o_ref):
    x = x_ref[...]
    D = x.shape[1]
    hb = _rms(x, n1_ref[...]).astype(BF16)
    acc = jnp.zeros(x.shape, F32)
    for n, br in enumerate((b0_ref, b1_ref, b2_ref, b3_ref)):
        gate = _sigmoid(_dot(hb, wg_ref[:, n * D:(n + 1) * D]) + bg_ref[:, n * D:(n + 1) * D])
        acc = acc + gate * _dot(br[...].astype(BF16), wb_ref[n])
    o_ref[...] = x + _dot(acc.astype(BF16), wo_ref[...])


def _merge(x2d, branches, layer, P, tm):
    T, D = x2d.shape
    row = lambda w: pl.BlockSpec((tm, w), lambda i: (i, 0))
    lay2 = lambda a, b: pl.BlockSpec((None, a, b), lambda i: (layer, 0, 0))
    return pl.pallas_call(
        _merge_kernel,
        grid=(T // tm,),
        in_specs=[row(D)] + [row(b.shape[1]) for b in branches]
                 + [lay2(1, D), lay2(D, N_BRANCH * D), lay2(1, N_BRANCH * D),
                    pl.BlockSpec((None, N_BRANCH, D // 2, D), lambda i: (layer, 0, 0, 0)), lay2(D, D)],
        out_specs=row(D),
        out_shape=jax.ShapeDtypeStruct((T, D), F32),
        compiler_params=pltpu.CompilerParams(dimension_semantics=("parallel",), vmem_limit_bytes=VMEM_LIMIT),
        name="merge",
    )(x2d, *branches, P['n1'], P['w_gate'], P['b_gate'], P['w_branch'], P['w_out'])


def _mlp_kernel(x_ref, n2_ref, w1_ref, w2_ref, o_ref):
    x = x_ref[...]
    u = jnp.maximum(_dot(_rms(x, n2_ref[...]).astype(BF16), w1_ref[...]), 0.0)
    o_ref[...] = x + _dot((u * u).astype(BF16), w2_ref[...])


def _mlp(x2d, layer, P, tm):
    T, D = x2d.shape
    dff = P['w1'].shape[2]
    row = pl.BlockSpec((tm, D), lambda i: (i, 0))
    return pl.pallas_call(
        _mlp_kernel,
        grid=(T // tm,),
        in_specs=[row, pl.BlockSpec((None, 1, D), lambda i: (layer, 0, 0)),
                  pl.BlockSpec((None, D, dff), lambda i: (layer, 0, 0)),
                  pl.BlockSpec((None, dff, D), lambda i: (layer, 0, 0))],
        out_specs=row,
        out_shape=jax.ShapeDtypeStruct((T, D), F32),
        compiler_params=pltpu.CompilerParams(dimension_semantics=("parallel",), vmem_limit_bytes=VMEM_LIMIT),
        name="mlp",
    )(x2d, P['n2'], P['w1'], P['w2'])


def _prep_params(norm1_g, w_in, gla_wa2, gla_ba, gla_norm_g, hgrn_lb_param, hgrn_norm_g, pool_w, pool_scale,
                 nsa_q_norm_g, nsa_k_norm_g, w_branch, w_gate, b_gate, w_out, norm2_g, mlp_w1, mlp_w2):
    depth = w_in.shape[0]
    sizes = (QK, QK, VW, VW, GLA_RANK, QK, QK, VW, VW, 512, H_D * HD_D) + (KVH_D * HD_D,) * 6 + (H_D * 3,)
    offs = [0]
    for s in sizes:
        offs.append(offs[-1] + s)
    col = lambda i: w_in[:, :, offs[i]:offs[i + 1]]
    order = [0, 1, 2, 3, 5, 6, 7, 8, 9, 10, 11, 12, 13, 14, 15, 16, 4, 17]
    pad = jnp.zeros(w_in.shape[:2] + (LANES - GLA_RANK - H_D * 3,), w_in.dtype)
    w_in_r = jnp.concatenate([col(i) for i in order] + [pad], axis=-1).astype(BF16)
    wa2 = jnp.concatenate([gla_wa2, jnp.zeros((depth, LANES - GLA_RANK, QK), gla_wa2.dtype)], axis=1).astype(BF16)
    r3 = lambda a: a.reshape(depth, 1, a.shape[-1])
    lane = jnp.arange(LANES)
    seg = jnp.where((lane[:, None] // HD_D) == (lane[None, :] // HD_D), 1.0 / HD_D, 0.0).astype(BF16)
    w_kvt = jnp.transpose(w_in_r[:, :, OFF_KV:OFF_TAIL], (0, 2, 1))
    return dict(
        n1=r3(norm1_g), w_in=w_in_r, w_kvt=w_kvt, wa2=wa2, ba=r3(gla_ba), lbp=hgrn_lb_param,
        qg=r3(jnp.tile(nsa_q_norm_g, (1, H_D))), kg=jnp.tile(nsa_k_norm_g, (1, 1, KVH_D)), seg=seg,
        kgc=nsa_k_norm_g.reshape(depth, 3, HD_D, 1),
        gla_g=gla_norm_g, hgrn_g=hgrn_norm_g, pool_w=pool_w.astype(BF16), pool_sc=r3(pool_scale),
        w_gate=w_gate.astype(BF16), b_gate=r3(b_gate), w_branch=w_branch.astype(BF16), w_out=w_out.astype(BF16),
        n2=r3(norm2_g), w1=mlp_w1.astype(BF16), w2=mlp_w2.astype(BF16))


def _overlap(n_cmp_rows, n_lanes):
    per = SEL_BLOCK // CMP_STRIDE
    ci = jnp.arange(n_cmp_rows)[:, None]
    sj = jnp.arange(n_lanes)[None, :]
    return jnp.where((ci // per == sj) | ((ci + 1) // per == sj), 1.0, 0.0).astype(BF16)


def _prep_consts(L, past):
    TQ = 256
    KC = 512
    kpos = jnp.arange(L).reshape(L // KC, 1, KC)
    exp_p = jnp.where(kpos // SEL_BLOCK == jnp.arange(LANES)[None, :, None], 1.0, 0.0).astype(BF16)
    n16_d = past // CMP_STRIDE
    n_sel_d = (past + 1 + SEL_BLOCK - 1) // SEL_BLOCK
    n_lane_d = -(-n_sel_d // LANES) * LANES
    r = jnp.arange(L)[:, None]
    c0 = jnp.arange(L // CMP_STRIDE)[None, :] * CMP_STRIDE
    pool32_p = jnp.where((r >= c0) & (r < c0 + CMP_BLOCK), 1.0, 0.0).astype(BF16)
    return dict(TQ=TQ, KC=KC, ovl_p=_overlap(L // CMP_STRIDE, L // SEL_BLOCK).T, exp_p=exp_p, pool32_p=pool32_p,
                ovl_d=_overlap(n16_d, n_lane_d), n_sel_d=n_sel_d)


def kernel(x_prompt, x_sample, state_gla, state_hgrn, state_pool, cache_cmp_kv, cache_slc_kv, cache_win_kv,
           page_table, norm1_g, w_in, gla_wa2, gla_ba, gla_norm_g, hgrn_lb_param, hgrn_norm_g, pool_w,
           pool_scale, nsa_q_norm_g, nsa_k_norm_g, w_branch, w_gate, b_gate, w_out, norm2_g, mlp_w1, mlp_w2):
    B, L, D = x_prompt.shape
    Bd = x_sample.shape[0]
    depth = w_in.shape[0]
    n_pages = page_table.shape[1]
    past = n_pages * PAGE_SIZE
    assert x_sample.shape[1] == 1 and L % 512 == 0 and L >= WINDOW + 256 and past % SEL_BLOCK == 0
    assert cache_win_kv.shape[2] == WINDOW and past >= WINDOW

    P = _prep_params(norm1_g, w_in, gla_wa2, gla_ba, gla_norm_g, hgrn_lb_param, hgrn_norm_g, pool_w, pool_scale,
                     nsa_q_norm_g, nsa_k_norm_g, w_branch, w_gate, b_gate, w_out, norm2_g, mlp_w1, mlp_w2)
    C = _prep_consts(L, past)
    tm_p = 256
    cb = 128
    bb = 8 if Bd % 8 == 0 else Bd
    keep = min(WINDOW, L)

    pps = 32 if n_pages % 32 == 0 else n_pages
    cmp_fm = _feature_major(cache_cmp_kv).astype(BF16)
    slc_fm = _feature_major(cache_slc_kv).astype(BF16)
    win_fm = _feature_major(cache_win_kv).astype(BF16)
    s16_all = _page_sums(cmp_fm, page_table, pps)

    xp = x_prompt.reshape(B * L, D)
    xs = x_sample.reshape(Bd, D)
    outs = {k: [] for k in ('gla_p', 'gla_s', 'hgrn_p', 'hgrn_s', 'pool_p', 'pool_s', 'cmp_p', 'cmp_s',
                            'slc_p', 'slc_s', 'win_p', 'win_s')}
    zero_prev = jnp.zeros((B, POOL_HIST + 1, 512), F32)
    rows5 = lambda a, n: a.reshape(n, -1, 2, KVH_D, HD_D)
    for l in range(depth):
        gqkg, gvr, hqkg, hvr, u, nq, cmp_t, slc_c, win_c, tail = _in_proj(xp, l, P, tm_p, seq_len=L, kc=C['KC'])
        b3 = lambda a: a.reshape(B, L, a.shape[-1])
        o_gla, s_gla = _gla_prompt(b3(gqkg), b3(gvr), P['gla_g'][l:l + 1], cb)
        o_hgrn, s_hgrn = _gla_prompt(b3(hqkg), b3(hvr), P['hgrn_g'][l:l + 1], cb)
        o_pool = _pool(b3(u), zero_prev, l, P, 0)
        kcvc = _cmp_prompt(cmp_t, l, P, C)
        o_nsa = _nsa_prompt(b3(nq), b3(tail), kcvc, slc_c, win_c, C)
        flat = lambda a: a.reshape(B * L, a.shape[-1])
        xp = _merge(xp, (flat(o_gla), flat(o_hgrn), flat(o_pool), flat(o_nsa)), l, P, tm_p)
        xp = _mlp(xp, l, P, tm_p)
        outs['gla_p'].append(s_gla)
        outs['hgrn_p'].append(s_hgrn)
        outs['pool_p'].append(b3(u)[:, L - POOL_HIST:])
        unchunk = lambda a: jnp.transpose(a, (0, 2, 1, 3)).reshape(B, KV_ROW, -1)
        outs['cmp_p'].append(cmp_t)
        outs['slc_p'].append(unchunk(slc_c))
        outs['win_p'].append(unchunk(win_c[:, (L - keep) // LANES:]))

        gqkg, gvr, hqkg, hvr, u, nq, cmp_r, slc_r, win_r, tail = _in_proj(xs, l, P, Bd)
        o_gla, s_gla = _gla_decode(gqkg, gvr, P['gla_g'][l], state_gla, l, bb)
        o_hgrn, s_hgrn = _gla_decode(hqkg, hvr, P['hgrn_g'][l], state_hgrn, l, bb)
        prev = state_pool[l]
        prev16 = jnp.concatenate([jnp.zeros((Bd, 1, 512), F32), prev], axis=1)
        o_pool = _pool(u.reshape(Bd, 1, 512), prev16, l, P, past).reshape(Bd, 512)
        ocmp, idx = _nsa_dec_select(nq, s16_all, l, P, C, past)
        o_nsa = _nsa_dec_attend(nq, tail, ocmp, idx, slc_r, win_r, slc_fm, win_fm, page_table, l, C)
        xs = _merge(xs, (o_gla, o_hgrn, o_pool, o_nsa), l, P, Bd)
        xs = _mlp(xs, l, P, Bd)
        outs['gla_s'].append(s_gla)
        outs['hgrn_s'].append(s_hgrn)
        outs['pool_s'].append(jnp.concatenate([prev[:, 1:], u.reshape(Bd, 1, 512)], axis=1))
        outs['cmp_s'].append(rows5(cmp_r, Bd))
        outs['slc_s'].append(rows5(slc_r, Bd))
        outs['win_s'].append(rows5(win_r, Bd))

    st = lambda k: jnp.stack(outs[k])

    def rows_p(k):
        a = st(k)
        return jnp.transpose(a.reshape(depth, B, 2, KVH_D, HD_D, a.shape[-1]), (0, 1, 5, 2, 3, 4))

    return (xp.reshape(B, L, D), xs.reshape(Bd, 1, D),
            st('gla_p'), st('gla_s'), st('hgrn_p'), st('hgrn_s'), st('pool_p'), st('pool_s'),
            rows_p('cmp_p'), st('cmp_s'), rows_p('slc_p'), st('slc_s'), rows_p('win_p'), st('win_s'))
```
